```python
import math
import jax, jax.numpy as jnp
from jax import lax
import numpy as np

D_MODEL = 1024
BATCH = 2
SEQ = 16384
DEPTH = 1
DEC_BATCH = 4
DEC_SEQ = 4096
PAST_LEN = 128

HEAD_DIM = 64
N_GQA_HEADS = 8
N_GQA_KV = 2
GQA_GROUP = N_GQA_HEADS // N_GQA_KV
N_DIFF_HEADS = 4
DIFF_V_DIM = 2 * HEAD_DIM
GQA_WIDTH = N_GQA_HEADS * HEAD_DIM
DIFF_WIDTH = N_DIFF_HEADS * DIFF_V_DIM
MIX_WIDTH = GQA_WIDTH + DIFF_WIDTH
GQA_Q_COLS = N_GQA_HEADS * HEAD_DIM
GQA_KV_COLS = N_GQA_KV * HEAD_DIM
DIFF_QK_COLS = N_DIFF_HEADS * 2 * HEAD_DIM
DIFF_V_COLS = N_DIFF_HEADS * DIFF_V_DIM
IN_WIDTH = GQA_Q_COLS + 2 * GQA_KV_COLS + 2 * DIFF_QK_COLS + DIFF_V_COLS
D_FF = -(-8 * D_MODEL // (3 * 256)) * 256
GRID_W = 64
Q_BLOCK = 128
NUM_BUCKETS = 32
MAX_DISTANCE = 128
ROPE_THETA = 10000.0
EPS = 1e-6
ATTN_SCALE = 1.0 / math.sqrt(HEAD_DIM)

kernel_name = 'hybrid_gqa_axialrope_diffattn_encoder'


def rmsnorm(x, g):
    xf = x.astype(jnp.float32)
    y = xf * lax.rsqrt(jnp.mean(xf * xf, axis=-1, keepdims=True) + EPS)
    return (y * g.astype(jnp.float32)).astype(x.dtype)


def axial_rope_tables(n):
    rows = n // GRID_W
    row = jnp.repeat(jnp.arange(rows), GRID_W).astype(jnp.float32)
    col = jnp.tile(jnp.arange(GRID_W), rows).astype(jnp.float32)
    half = HEAD_DIM // 2
    inv = ROPE_THETA ** (-jnp.arange(0, half, 2, dtype=jnp.float32) / half)
    ang_r = row[:, None] * inv[None, :]
    ang_c = col[:, None] * inv[None, :]
    ang = jnp.concatenate([ang_r, ang_r, ang_c, ang_c], axis=-1)
    return jnp.cos(ang), jnp.sin(ang)


def apply_rope(x, cos, sin):
    xf = x.astype(jnp.float32)
    xs = xf.reshape(*xf.shape[:-1], 2, 2, HEAD_DIM // 4)
    rot = jnp.stack([-xs[..., 1, :], xs[..., 0, :]], axis=-2).reshape(xf.shape)
    bshape = (1, cos.shape[0]) + (1,) * (x.ndim - 3) + (HEAD_DIM,)
    return (xf * cos.reshape(bshape) + rot * sin.reshape(bshape)).astype(x.dtype)


def rel_bucket(rel):
    half = NUM_BUCKETS // 2
    max_exact = half // 2
    n = jnp.abs(rel)
    nf = jnp.maximum(n, max_exact).astype(jnp.float32)
    large = max_exact + (jnp.log(nf / max_exact) / math.log(MAX_DISTANCE / max_exact)
                         * (half - max_exact)).astype(jnp.int32)
    large = jnp.minimum(large, half - 1)
    return jnp.where(rel > 0, half, 0) + jnp.where(n < max_exact, n, large)


def gqa_attention(q, k, v):
    b, n = q.shape[0], q.shape[1]
    nblk = n // Q_BLOCK
    qb = jnp.moveaxis(q.reshape(b, nblk, Q_BLOCK, *q.shape[2:]), 1, 0)

    def one(qblk):
        s = jnp.einsum('bqkgd,bskd->bkgqs', qblk, k, preferred_element_type=jnp.float32) * ATTN_SCALE
        p = jax.nn.softmax(s, axis=-1).astype(v.dtype)
        return jnp.einsum('bkgqs,bskd->bqkgd', p, v)

    out = lax.map(one, qb)
    return jnp.moveaxis(out, 0, 1).reshape(b, n, GQA_WIDTH)


def diff_attention(q, k, v, lam, rel_bias):
    b, n = q.shape[0], q.shape[1]
    nblk = n // Q_BLOCK
    qb = jnp.moveaxis(q.reshape(b, nblk, Q_BLOCK, *q.shape[2:]), 1, 0)
    starts = jnp.arange(nblk) * Q_BLOCK
    kpos = jnp.arange(n)

    def one(args):
        qblk, start = args
        qpos = start + jnp.arange(Q_BLOCK)
        bucket = rel_bucket(kpos[None, :] - qpos[:, None])
        bias = jnp.moveaxis(rel_bias[bucket], -1, 0).astype(jnp.float32)
        s = jnp.einsum('bqhjd,bshjd->bhjqs', qblk, k, preferred_element_type=jnp.float32) * ATTN_SCALE
        p = jax.nn.softmax(s + bias[None, :, None], axis=-1)
        a = p[:, :, 0] - lam * p[:, :, 1]
        return jnp.einsum('bhqs,bshe->bqhe', a.astype(v.dtype), v)

    out = lax.map(one, (qb, starts))
    return jnp.moveaxis(out, 0, 1).reshape(b, n, N_DIFF_HEADS, DIFF_V_DIM)


def encoder_layer(x, c, layer_idx, rel_bias, w_ada, b_ada, g_pre_mix, w_in, g_q, g_k,
                  lam_q1, lam_k1, lam_q2, lam_k2, g_subln, w_out, g_post_mix,
                  g_pre_ffn, w_gu, w_down, g_post_ffn):
    b, n, _ = x.shape
    mod = jax.nn.silu(c) @ w_ada + b_ada
    sh1, sc1, gt1, sh2, sc2, gt2 = jnp.split(mod[:, None, :], 6, axis=-1)

    h = rmsnorm(x, g_pre_mix) * (1 + sc1) + sh1
    proj = h @ w_in
    o1 = GQA_Q_COLS
    o2 = o1 + GQA_KV_COLS
    o3 = o2 + GQA_KV_COLS
    o4 = o3 + DIFF_QK_COLS
    o5 = o4 + DIFF_QK_COLS
    qa = proj[..., :o1].reshape(b, n, N_GQA_KV, GQA_GROUP, HEAD_DIM)
    ka = proj[..., o1:o2].reshape(b, n, N_GQA_KV, HEAD_DIM)
    va = proj[..., o2:o3].reshape(b, n, N_GQA_KV, HEAD_DIM)
    qd = proj[..., o3:o4].reshape(b, n, N_DIFF_HEADS, 2, HEAD_DIM)
    kd = proj[..., o4:o5].reshape(b, n, N_DIFF_HEADS, 2, HEAD_DIM)
    vd = proj[..., o5:].reshape(b, n, N_DIFF_HEADS, DIFF_V_DIM)

    cos, sin = axial_rope_tables(n)
    qa = apply_rope(rmsnorm(qa, g_q), cos, sin)
    ka = apply_rope(rmsnorm(ka, g_k), cos, sin)
    out_a = gqa_attention(qa, ka, va)

    lam_init = 0.8 - 0.6 * math.exp(-0.3 * layer_idx)
    lam = (jnp.exp(jnp.sum(lam_q1.astype(jnp.float32) * lam_k1.astype(jnp.float32)))
           - jnp.exp(jnp.sum(lam_q2.astype(jnp.float32) * lam_k2.astype(jnp.float32))) + lam_init)
    out_d = diff_attention(qd, kd, vd, lam, rel_bias)
    out_d = (rmsnorm(out_d, g_subln) * (1.0 - lam_init)).reshape(b, n, DIFF_WIDTH)

    mix = jnp.concatenate([out_a, out_d], axis=-1) @ w_out
    x = x + gt1 * rmsnorm(mix, g_post_mix)

    h = rmsnorm(x, g_pre_ffn) * (1 + sc2) + sh2
    gate, up = jnp.split(h @ w_gu, 2, axis=-1)
    f = (jax.nn.silu(gate) * up) @ w_down
    return x + gt2 * rmsnorm(f, g_post_ffn)


def trunk(x, c, rel_bias, w_ada, b_ada, g_pre_mix, w_in, g_q, g_k, lam_q1, lam_k1,
          lam_q2, lam_k2, g_subln, w_out, g_post_mix, g_pre_ffn, w_gu, w_down, g_post_ffn):
    for l in range(DEPTH):
        x = encoder_layer(x, c, l, rel_bias, w_ada[l], b_ada[l], g_pre_mix[l], w_in[l],
                          g_q[l], g_k[l], lam_q1[l], lam_k1[l], lam_q2[l], lam_k2[l],
                          g_subln[l], w_out[l], g_post_mix[l], g_pre_ffn[l], w_gu[l],
                          w_down[l], g_post_ffn[l])
    return x


def setup_inputs(seed: int = 0) -> dict:
    key = jax.random.key(seed)
    ks = jax.random.split(key, 24)
    f32 = jnp.float32

    def nrm(k, shape, scale):
        return jax.random.normal(k, shape, f32) * scale

    def gain(k, shape):
        return 1.0 + 0.05 * jax.random.normal(k, shape, f32)

    return {
        'x_prompt': nrm(ks[0], (BATCH, SEQ, D_MODEL), 1.0),
        'x_sample': nrm(ks[1], (DEC_BATCH, DEC_SEQ, D_MODEL), 1.0),
        'c_prompt': nrm(ks[2], (BATCH, D_MODEL), 1.0),
        'c_sample': nrm(ks[3], (DEC_BATCH, D_MODEL), 1.0),
        'rel_bias': nrm(ks[4], (NUM_BUCKETS, N_DIFF_HEADS), 0.5),
        'w_ada': nrm(ks[5], (DEPTH, D_MODEL, 6 * D_MODEL), 0.5 * D_MODEL ** -0.5),
        'b_ada': nrm(ks[6], (DEPTH, 6 * D_MODEL), 0.01),
        'g_pre_mix': gain(ks[7], (DEPTH, D_MODEL)),
        'w_in': nrm(ks[8], (DEPTH, D_MODEL, IN_WIDTH), D_MODEL ** -0.5),
        'g_q': gain(ks[9], (DEPTH, HEAD_DIM)),
        'g_k': gain(ks[10], (DEPTH, HEAD_DIM)),
        'lam_q1': nrm(ks[11], (DEPTH, HEAD_DIM), 0.1),
        'lam_k1': nrm(ks[12], (DEPTH, HEAD_DIM), 0.1),
        'lam_q2': nrm(ks[13], (DEPTH, HEAD_DIM), 0.1),
        'lam_k2': nrm(ks[14], (DEPTH, HEAD_DIM), 0.1),
        'g_subln': gain(ks[15], (DEPTH, DIFF_V_DIM)),
        'w_out': nrm(ks[16], (DEPTH, MIX_WIDTH, D_MODEL), MIX_WIDTH ** -0.5),
        'g_post_mix': gain(ks[17], (DEPTH, D_MODEL)),
        'g_pre_ffn': gain(ks[18], (DEPTH, D_MODEL)),
        'w_gu': nrm(ks[19], (DEPTH, D_MODEL, 2 * D_FF), D_MODEL ** -0.5),
        'w_down': nrm(ks[20], (DEPTH, D_FF, D_MODEL), D_FF ** -0.5),
        'g_post_ffn': gain(ks[21], (DEPTH, D_MODEL)),
    }


def reference(x_prompt, x_sample, c_prompt, c_sample, rel_bias, w_ada, b_ada, g_pre_mix,
              w_in, g_q, g_k, lam_q1, lam_k1, lam_q2, lam_k2, g_subln, w_out, g_post_mix,
              g_pre_ffn, w_gu, w_down, g_post_ffn):
    y_prompt = trunk(x_prompt, c_prompt, rel_bias, w_ada, b_ada, g_pre_mix, w_in, g_q, g_k,
                     lam_q1, lam_k1, lam_q2, lam_k2, g_subln, w_out, g_post_mix,
                     g_pre_ffn, w_gu, w_down, g_post_ffn)
    y_sample = trunk(x_sample, c_sample, rel_bias, w_ada, b_ada, g_pre_mix, w_in, g_q, g_k,
                     lam_q1, lam_k1, lam_q2, lam_k2, g_subln, w_out, g_post_mix,
                     g_pre_ffn, w_gu, w_down, g_post_ffn)
    return (y_prompt, y_sample)
```

```python
import functools
import math

import jax
import jax.numpy as jnp
from jax import lax
from jax.experimental import pallas as pl
from jax.experimental.pallas import tpu as pltpu

F32 = jnp.float32
BF16 = jnp.bfloat16

HEAD_DIM = 64
N_GQA_HEADS = 8
N_GQA_KV = 2
GQA_GROUP = N_GQA_HEADS // N_GQA_KV
N_DIFF_HEADS = 4
GRID_W = 64
NUM_BUCKETS = 32
MAX_DISTANCE = 128
ROPE_THETA = 10000.0
EPS = 1e-6
ATTN_SCALE = 1.0 / math.sqrt(HEAD_DIM)

LANES = 128
GQA_Q_COLS = N_GQA_HEADS * HEAD_DIM
GQA_KV_COLS = N_GQA_KV * HEAD_DIM
DIFF_COLS = N_DIFF_HEADS * 2 * HEAD_DIM

QA_BLK = 0
KA_BLK = QA_BLK + GQA_Q_COLS // LANES
VA_BLK = KA_BLK + N_GQA_KV
QD_BLK = VA_BLK + N_GQA_KV
KD_BLK = QD_BLK + N_DIFF_HEADS
VD_BLK = KD_BLK + N_DIFF_HEADS
ACT_COLS = (VD_BLK + N_DIFF_HEADS) * LANES

MOD_ROWS = 8
M_INIT = -0.5 * float(jnp.finfo(jnp.float32).max)

VMEM_LIMIT = 56 * 1024 * 1024


def _rms(x, g):
    return x * lax.rsqrt(jnp.mean(x * x, axis=-1, keepdims=True) + EPS) * g


def _mod_kernel(c_ref, w_ref, b_ref, o_ref):
    c = c_ref[...]
    a = c * (1.0 / (1.0 + jnp.exp(-c)))
    a_hi = a.astype(BF16)
    a_lo = (a - a_hi.astype(F32)).astype(BF16)
    w = w_ref[...]
    w_hi = w.astype(BF16)
    w_lo = (w - w_hi.astype(F32)).astype(BF16)
    acc = jnp.dot(a_hi, w_hi, preferred_element_type=F32)
    acc += jnp.dot(a_lo, w_hi, preferred_element_type=F32)
    acc += jnp.dot(a_hi, w_lo, preferred_element_type=F32)
    o_ref[...] = acc + b_ref[...]


def _modulation(c_all, w_ada, b_ada):
    rows, d = c_all.shape
    n_out = w_ada.shape[1]
    tn = 1536
    return pl.pallas_call(
        _mod_kernel,
        grid=(n_out // tn,),
        in_specs=[
            pl.BlockSpec((rows, d), lambda j: (0, 0)),
            pl.BlockSpec((d, tn), lambda j: (0, j)),
            pl.BlockSpec((1, tn), lambda j: (0, j)),
        ],
        out_specs=pl.BlockSpec((rows, tn), lambda j: (0, j)),
        out_shape=jax.ShapeDtypeStruct((rows, n_out), F32),
        compiler_params=pltpu.CompilerParams(
            dimension_semantics=("arbitrary",), vmem_limit_bytes=VMEM_LIMIT),
        name="modulation",
    )(c_all, w_ada, b_ada.reshape(1, n_out))


def _inproj_kernel(x_ref, mod_ref, g_ref, w_ref, gq_ref, gk_ref, cos_ref, sin_ref, seg_ref,
                   o_ref):
    x = x_ref[0]
    sh = mod_ref[0, 0:1, :]
    sc = mod_ref[0, 1:2, :]
    h = _rms(x, g_ref[...]) * (1.0 + sc) + sh
    proj = jnp.dot(h.astype(BF16), w_ref[...], preferred_element_type=F32)

    tm = x.shape[0]
    cos = cos_ref[...]
    sin = sin_ref[...]
    seg = seg_ref[...]
    lane = lax.broadcasted_iota(jnp.int32, (tm, LANES), 1)
    first_half = (lane & 16) == 0
    low = lane < HEAD_DIM

    def norm_rope(t, g):
        ss = jnp.dot((t * t).astype(BF16), seg, preferred_element_type=F32)
        tn = t * lax.rsqrt(ss * (1.0 / HEAD_DIM) + EPS) * g
        rot = jnp.where(first_half, pltpu.roll(tn, LANES - 16, 1), pltpu.roll(tn, 16, 1))
        return tn * cos + rot * sin

    def put(blk, val):
        o_ref[0, :, blk * LANES:(blk + 1) * LANES] = val.astype(BF16)

    col = 0
    for j in range(GQA_Q_COLS // LANES):
        put(QA_BLK + j, norm_rope(proj[:, col:col + LANES], gq_ref[...]) * ATTN_SCALE)
        col += LANES
    kr = norm_rope(proj[:, col:col + LANES], gk_ref[...])
    col += LANES
    kr_sw = pltpu.roll(kr, HEAD_DIM, 1)
    put(KA_BLK + 0, jnp.where(low, kr, kr_sw))
    put(KA_BLK + 1, jnp.where(low, kr_sw, kr))
    va = proj[:, col:col + LANES]
    col += LANES
    va_sw = pltpu.roll(va, HEAD_DIM, 1)
    put(VA_BLK + 0, jnp.where(low, va, 1.0))
    put(VA_BLK + 1, jnp.where(low, va_sw, 1.0))
    for j in range(N_DIFF_HEADS):
        put(QD_BLK + j, proj[:, col:col + LANES] * ATTN_SCALE)
        col += LANES
    for j in range(N_DIFF_HEADS):
        put(KD_BLK + j, proj[:, col:col + LANES])
        col += LANES
    for j in range(N_DIFF_HEADS):
        put(VD_BLK + j, proj[:, col:col + LANES])
        col += LANES


def _in_projection(x, mod3, g_pre, w_in_bf, gq2, gk2, cos2, sin2, seg, tm=512):
    b, n, d = x.shape
    in_w = w_in_bf.shape[1]
    const = lambda bi, i: (0, 0)
    return pl.pallas_call(
        _inproj_kernel,
        grid=(b, n // tm),
        in_specs=[
            pl.BlockSpec((1, tm, d), lambda bi, i: (bi, i, 0)),
            pl.BlockSpec((1, 6, d), lambda bi, i: (bi, 0, 0)),
            pl.BlockSpec((1, d), const),
            pl.BlockSpec((d, in_w), const),
            pl.BlockSpec((1, LANES), const),
            pl.BlockSpec((1, LANES), const),
            pl.BlockSpec((tm, LANES), lambda bi, i: (i, 0)),
            pl.BlockSpec((tm, LANES), lambda bi, i: (i, 0)),
            pl.BlockSpec((LANES, LANES), const),
        ],
        out_specs=pl.BlockSpec((1, tm, ACT_COLS), lambda bi, i: (bi, i, 0)),
        out_shape=jax.ShapeDtypeStruct((b, n, ACT_COLS), BF16),
        compiler_params=pltpu.CompilerParams(
            dimension_semantics=("arbitrary", "arbitrary"), vmem_limit_bytes=VMEM_LIMIT),
        name="in_projection",
    )(x, mod3, g_pre, w_in_bf, gq2, gk2, cos2, sin2, seg)


def _flash_update(q, kc, vc, m_ref, acc_ref, rows, bias=None, shift=None, l_ref=None):
    s = lax.dot_general(q, kc, (((1,), (1,)), ((), ())), preferred_element_type=F32)
    if bias is not None:
        s = s + bias
    m_prev = m_ref[rows, :]
    m_cur = jnp.max(s, axis=-1, keepdims=True)
    if shift is not None:
        m_cur = m_cur + shift
    m_new = jnp.maximum(m_prev, m_cur)
    alpha = jnp.exp(m_prev - m_new)
    p = jnp.exp(s - (m_new if shift is None else m_new - shift))
    pv = jnp.dot(p.astype(BF16), vc, preferred_element_type=F32)
    acc_ref[rows, :] = alpha * acc_ref[rows, :] + pv
    if l_ref is not None:
        l_ref[rows, :] = alpha * l_ref[rows, :] + jnp.sum(p, axis=-1, keepdims=True)
    m_ref[rows, :] = m_new


def _gqa_kernel(q_ref, k_ref, v_ref, o_ref, qz_ref, m_ref, acc_ref, *, tq, tk):
    n = k_ref.shape[1]
    mrows = GQA_GROUP * tq
    lane = lax.broadcasted_iota(jnp.int32, (tq, LANES), 1)
    low = lane < HEAD_DIM
    for g in range(GQA_GROUP):
        slab = q_ref[0, :, (g // 2) * LANES:(g // 2 + 1) * LANES]
        keep = low if g % 2 == 0 else jnp.logical_not(low)
        qz_ref[g * tq:(g + 1) * tq, :] = jnp.where(keep, slab, jnp.zeros_like(slab))
    m_ref[...] = jnp.full(m_ref.shape, M_INIT, F32)
    acc_ref[...] = jnp.zeros(acc_ref.shape, F32)

    half = mrows // 2
    blocks = (slice(0, half), slice(half, mrows))

    def body(c, carry):
        start = pl.multiple_of(c * tk, tk)
        kc = k_ref[0, pl.ds(start, tk), :]
        vc = v_ref[0, pl.ds(start, tk), :]
        for rows in blocks:
            _flash_update(qz_ref[rows, :], kc, vc, m_ref, acc_ref, rows)
        return carry

    lax.fori_loop(0, n // tk, body, 0)

    outs = []
    for g in range(GQA_GROUP):
        a = acc_ref[g * tq:(g + 1) * tq, :]
        outs.append(a / pltpu.roll(a, HEAD_DIM, 1))
    for j in range(GQA_GROUP // 2):
        pair = jnp.where(low, outs[2 * j], pltpu.roll(outs[2 * j + 1], HEAD_DIM, 1))
        o_ref[0, :, j * LANES:(j + 1) * LANES] = pair.astype(o_ref.dtype)


def _gqa_attention(act, tq=256, tk=512):
    b, n, _ = act.shape
    width = GQA_GROUP * HEAD_DIM
    mrows = GQA_GROUP * tq
    return pl.pallas_call(
        functools.partial(_gqa_kernel, tq=tq, tk=tk),
        grid=(b, N_GQA_KV, n // tq),
        in_specs=[
            pl.BlockSpec((1, tq, width), lambda bi, h, i: (bi, i, h)),
            pl.BlockSpec((1, n, LANES), lambda bi, h, i: (bi, 0, KA_BLK + h)),
            pl.BlockSpec((1, n, LANES), lambda bi, h, i: (bi, 0, VA_BLK + h)),
        ],
        out_specs=pl.BlockSpec((1, tq, width), lambda bi, h, i: (bi, i, h)),
        out_shape=jax.ShapeDtypeStruct((b, n, GQA_Q_COLS), BF16),
        scratch_shapes=[
            pltpu.VMEM((mrows, LANES), BF16),
            pltpu.VMEM((mrows, 1), F32),
            pltpu.VMEM((mrows, LANES), F32),
        ],
        compiler_params=pltpu.CompilerParams(
            dimension_semantics=("arbitrary", "arbitrary", "arbitrary"),
            vmem_limit_bytes=VMEM_LIMIT),
        name="gqa_attention",
    )(act, act, act)


def _rel_bucket(rel):
    half = NUM_BUCKETS // 2
    max_exact = half // 2
    n = jnp.minimum(jnp.abs(rel), MAX_DISTANCE)
    n2 = n * n
    large = jnp.full(rel.shape, max_exact, jnp.int32)
    for k in range(1, half - max_exact):
        large = large + (n2 >= (max_exact * max_exact) * (2 ** k)).astype(jnp.int32)
    return jnp.where(rel > 0, half, 0) + jnp.where(n < max_exact, n, large)


def _diff_kernel(rb_ref, lam_ref, gs_ref, q_ref, k_ref, v_ref, o_ref,
                 qz_ref, m_ref, l_ref, acc_ref, bias_ref, *, t, lam_init):
    h = pl.program_id(0)
    bi = pl.program_id(1)
    i = pl.program_id(2)
    n = k_ref.shape[1]
    nc = n // t
    half_b = NUM_BUCKETS // 2

    @pl.when(jnp.logical_and(bi == 0, i == 0))
    def _():
        def fill(r, carry):
            r0 = pl.multiple_of(r * 8, 8)
            row = r0 + lax.broadcasted_iota(jnp.int32, (8, t), 0)
            col = lax.broadcasted_iota(jnp.int32, (8, t), 1)
            for d in (-1, 0, 1):
                bucket = _rel_bucket(d * t + col - row)
                val = jnp.zeros((8, t), F32)
                for bkt in range(NUM_BUCKETS):
                    val = jnp.where(bucket == bkt, rb_ref[bkt, h], val)
                bias_ref[d + 1, pl.ds(r0, 8), :] = val
            return carry
        lax.fori_loop(0, t // 8, fill, 0)

    lane = lax.broadcasted_iota(jnp.int32, (t, LANES), 1)
    low = lane < HEAD_DIM
    q = q_ref[0]
    zero = jnp.zeros_like(q)
    qz_ref[0:t, :] = jnp.where(low, q, zero)
    qz_ref[t:2 * t, :] = jnp.where(low, zero, q)
    m_ref[...] = jnp.full(m_ref.shape, M_INIT, F32)
    l_ref[...] = jnp.zeros(l_ref.shape, F32)
    acc_ref[...] = jnp.zeros(acc_ref.shape, F32)

    blocks = (slice(0, t), slice(t, 2 * t))

    def step(c, bias=None, shift=None):
        start = pl.multiple_of(c * t, t)
        kc = k_ref[0, pl.ds(start, t), :]
        vc = v_ref[0, pl.ds(start, t), :]
        for rows in blocks:
            _flash_update(qz_ref[rows, :], kc, vc, m_ref, acc_ref, rows,
                          bias=bias, shift=shift, l_ref=l_ref)

    def far_before(c, carry):
        step(c, shift=rb_ref[half_b - 1, h])
        return carry

    def near(c, carry):
        step(c, bias=bias_ref[c - i + 1])
        return carry

    def far_after(c, carry):
        step(c, shift=rb_ref[NUM_BUCKETS - 1, h])
        return carry

    lax.fori_loop(0, jnp.maximum(i - 1, 0), far_before, 0)
    lax.fori_loop(jnp.maximum(i - 1, 0), jnp.minimum(i + 2, nc), near, 0)
    lax.fori_loop(jnp.minimum(i + 2, nc), nc, far_after, 0)

    lv = lam_ref[...]
    lam = (jnp.exp(jnp.sum(lv[0:1, :] * lv[1:2, :], axis=-1, keepdims=True))
           - jnp.exp(jnp.sum(lv[2:3, :] * lv[3:4, :], axis=-1, keepdims=True)) + lam_init)
    o1 = acc_ref[0:t, :] / l_ref[0:t, :]
    o2 = acc_ref[t:2 * t, :] / l_ref[t:2 * t, :]
    o = o1 - lam * o2
    o_ref[0] = (_rms(o, gs_ref[...]) * (1.0 - lam_init)).astype(o_ref.dtype)


def _diff_attention(act, rel_bias, lam_rows, g_subln, lam_init, t=512):
    b, n, _ = act.shape
    return pl.pallas_call(
        functools.partial(_diff_kernel, t=t, lam_init=lam_init),
        grid=(N_DIFF_HEADS, b, n // t),
        in_specs=[
            pl.BlockSpec(memory_space=pltpu.SMEM),
            pl.BlockSpec((MOD_ROWS, LANES), lambda h, bi, i: (0, 0)),
            pl.BlockSpec((1, LANES), lambda h, bi, i: (0, 0)),
            pl.BlockSpec((1, t, LANES), lambda h, bi, i: (bi, i, QD_BLK + h)),
            pl.BlockSpec((1, n, LANES), lambda h, bi, i: (bi, 0, KD_BLK + h)),
            pl.BlockSpec((1, n, LANES), lambda h, bi, i: (bi, 0, VD_BLK + h)),
        ],
        out_specs=pl.BlockSpec((1, t, LANES), lambda h, bi, i: (bi, i, h)),
        out_shape=jax.ShapeDtypeStruct((b, n, DIFF_COLS), BF16),
        scratch_shapes=[
            pltpu.VMEM((2 * t, LANES), BF16),
            pltpu.VMEM((2 * t, 1), F32),
            pltpu.VMEM((2 * t, 1), F32),
            pltpu.VMEM((2 * t, LANES), F32),
            pltpu.VMEM((3, t, t), F32),
        ],
        compiler_params=pltpu.CompilerParams(
            dimension_semantics=("arbitrary", "arbitrary", "arbitrary"),
            vmem_limit_bytes=VMEM_LIMIT),
        name="diff_attention",
    )(rel_bias, lam_rows, g_subln, act, act, act)


def _ffn_kernel(x_ref, oa_ref, od_ref, mod_ref, wout_ref, gpm_ref, gpf_ref, wgu_ref, wdn_ref,
                gpo_ref, y_ref):
    x = x_ref[0]
    gt1 = mod_ref[0, 2:3, :]
    sh2 = mod_ref[0, 3:4, :]
    sc2 = mod_ref[0, 4:5, :]
    gt2 = mod_ref[0, 5:6, :]
    wa = oa_ref.shape[2]
    mix = jnp.dot(oa_ref[0], wout_ref[0:wa, :], preferred_element_type=F32)
    mix += jnp.dot(od_ref[0], wout_ref[wa:, :], preferred_element_type=F32)
    x1 = x + gt1 * _rms(mix, gpm_ref[...])
    h = _rms(x1, gpf_ref[...]) * (1.0 + sc2) + sh2
    gu = jnp.dot(h.astype(BF16), wgu_ref[...], preferred_element_type=F32)
    d_ff = wdn_ref.shape[0]
    gate = gu[:, :d_ff]
    up = gu[:, d_ff:]
    act = gate * (1.0 / (1.0 + jnp.exp(-gate))) * up
    f = jnp.dot(act.astype(BF16), wdn_ref[...], preferred_element_type=F32)
    y_ref[0] = x1 + gt2 * _rms(f, gpo_ref[...])


def _out_ffn(x, out_a, out_d, mod3, w_out_bf, g_post_mix, g_pre_ffn, w_gu_bf, w_down_bf,
             g_post_ffn, tm=256):
    b, n, d = x.shape
    const = lambda bi, i: (0, 0)
    once = pl.Buffered(1)
    return pl.pallas_call(
        _ffn_kernel,
        grid=(b, n // tm),
        in_specs=[
            pl.BlockSpec((1, tm, d), lambda bi, i: (bi, i, 0)),
            pl.BlockSpec((1, tm, out_a.shape[2]), lambda bi, i: (bi, i, 0)),
            pl.BlockSpec((1, tm, out_d.shape[2]), lambda bi, i: (bi, i, 0)),
            pl.BlockSpec((1, 6, d), lambda bi, i: (bi, 0, 0)),
            pl.BlockSpec(w_out_bf.shape, const, pipeline_mode=once),
            pl.BlockSpec((1, d), const),
            pl.BlockSpec((1, d), const),
            pl.BlockSpec(w_gu_bf.shape, const, pipeline_mode=once),
            pl.BlockSpec(w_down_bf.shape, const, pipeline_mode=once),
            pl.BlockSpec((1, d), const),
        ],
        out_specs=pl.BlockSpec((1, tm, d), lambda bi, i: (bi, i, 0)),
        out_shape=jax.ShapeDtypeStruct((b, n, d), F32),
        compiler_params=pltpu.CompilerParams(
            dimension_semantics=("arbitrary", "arbitrary"), vmem_limit_bytes=VMEM_LIMIT),
        name="out_ffn",
    )(x, out_a, out_d, mod3, w_out_bf, g_post_mix, g_pre_ffn, w_gu_bf, w_down_bf, g_post_ffn)


def _rope_tables(n):
    rows = n // GRID_W
    row = jnp.repeat(jnp.arange(rows), GRID_W).astype(F32)
    col = jnp.tile(jnp.arange(GRID_W), rows).astype(F32)
    half = HEAD_DIM // 2
    inv = ROPE_THETA ** (-jnp.arange(0, half, 2, dtype=F32) / half)
    ang_r = row[:, None] * inv[None, :]
    ang_c = col[:, None] * inv[None, :]
    ang = jnp.concatenate([ang_r, ang_r, ang_c, ang_c], axis=-1)
    sign = jnp.tile(jnp.concatenate([-jnp.ones((16,), F32), jnp.ones((16,), F32)]), 2)
    cos = jnp.cos(ang)
    sin = jnp.sin(ang) * sign[None, :]
    return jnp.tile(cos, (1, 2)), jnp.tile(sin, (1, 2))


def kernel(x_prompt, x_sample, c_prompt, c_sample, rel_bias, w_ada, b_ada, g_pre_mix, w_in,
           g_q, g_k, lam_q1, lam_k1, lam_q2, lam_k2, g_subln, w_out, g_post_mix, g_pre_ffn,
           w_gu, w_down, g_post_ffn):
    depth = w_ada.shape[0]
    d = x_prompt.shape[-1]
    xs = [x_prompt, x_sample]
    cs = [c_prompt, c_sample]
    n_c = sum(c.shape[0] for c in cs)
    assert n_c <= MOD_ROWS
    c_all = jnp.concatenate(cs + [jnp.zeros((MOD_ROWS - n_c, d), F32)], axis=0)
    seg = (jnp.arange(LANES)[:, None] // HEAD_DIM
           == jnp.arange(LANES)[None, :] // HEAD_DIM).astype(BF16)
    tables = [_rope_tables(x.shape[1]) for x in xs]

    for l in range(depth):
        lam_init = 0.8 - 0.6 * math.exp(-0.3 * l)
        mod = _modulation(c_all, w_ada[l], b_ada[l]).reshape(MOD_ROWS, 6, d)
        w_in_bf = w_in[l].astype(BF16)
        w_out_bf = w_out[l].astype(BF16)
        w_gu_bf = w_gu[l].astype(BF16)
        w_down_bf = w_down[l].astype(BF16)
        gq2 = jnp.tile(g_q[l], LANES // HEAD_DIM).reshape(1, LANES)
        gk2 = jnp.tile(g_k[l], LANES // HEAD_DIM).reshape(1, LANES)
        lam_rows = jnp.zeros((MOD_ROWS, LANES), F32).at[0:4, 0:HEAD_DIM].set(
            jnp.stack([lam_q1[l], lam_k1[l], lam_q2[l], lam_k2[l]]))
        new_xs = []
        row0 = 0
        for x, (cos2, sin2) in zip(xs, tables):
            mod3 = mod[row0:row0 + x.shape[0]]
            row0 += x.shape[0]
            act = _in_projection(x, mod3, g_pre_mix[l].reshape(1, d), w_in_bf, gq2, gk2,
                                 cos2, sin2, seg)
            out_a = _gqa_attention(act)
            out_d = _diff_attention(act, rel_bias, lam_rows, g_subln[l].reshape(1, LANES),
                                    lam_init)
            new_xs.append(_out_ffn(x, out_a, out_d, mod3, w_out_bf,
                                   g_post_mix[l].reshape(1, d), g_pre_ffn[l].reshape(1, d),
                                   w_gu_bf, w_down_bf, g_post_ffn[l].reshape(1, d)))
        xs = new_xs
    return tuple(xs)
```

```python
import functools
import math

import jax
import jax.numpy as jnp
from jax import lax
from jax.experimental import pallas as pl
from jax.experimental.pallas import tpu as pltpu

F32 = jnp.float32
BF16 = jnp.bfloat16

HEAD_DIM = 64
N_GQA_HEADS = 8
N_GQA_KV = 2
GQA_GROUP = N_GQA_HEADS // N_GQA_KV
N_DIFF_HEADS = 4
GRID_W = 64
NUM_BUCKETS = 32
MAX_DISTANCE = 128
ROPE_THETA = 10000.0
EPS = 1e-6
ATTN_SCALE = 1.0 / math.sqrt(HEAD_DIM)
LOG2E = 1.4426950408889634
Q_SCALE = ATTN_SCALE * LOG2E

LANES = 128
GQA_Q_COLS = N_GQA_HEADS * HEAD_DIM
GQA_KV_COLS = N_GQA_KV * HEAD_DIM
DIFF_COLS = N_DIFF_HEADS * 2 * HEAD_DIM

QA_BLK = 0
KA_BLK = QA_BLK + GQA_Q_COLS // LANES
VA_BLK = KA_BLK + N_GQA_KV
QD_BLK = VA_BLK + N_GQA_KV
KD_BLK = QD_BLK + N_DIFF_HEADS
VD_BLK = KD_BLK + N_DIFF_HEADS
ACT_COLS = (VD_BLK + N_DIFF_HEADS) * LANES

MOD_ROWS = 8
M_INIT = -0.5 * float(jnp.finfo(jnp.float32).max)

VMEM_LIMIT = 56 * 1024 * 1024


def _rms(x, g):
    return x * lax.rsqrt(jnp.mean(x * x, axis=-1, keepdims=True) + EPS) * g


def _mod_kernel(c_ref, w_ref, b_ref, o_ref):
    c = c_ref[...]
    a = c * (1.0 / (1.0 + jnp.exp(-c)))
    a_hi = a.astype(BF16)
    a_lo = (a - a_hi.astype(F32)).astype(BF16)
    w = w_ref[...]
    w_hi = w.astype(BF16)
    w_lo = (w - w_hi.astype(F32)).astype(BF16)
    acc = jnp.dot(a_hi, w_hi, preferred_element_type=F32)
    acc += jnp.dot(a_lo, w_hi, preferred_element_type=F32)
    acc += jnp.dot(a_hi, w_lo, preferred_element_type=F32)
    o_ref[...] = acc + b_ref[...]


def _modulation(c_all, w_ada, b_ada):
    rows, d = c_all.shape
    n_out = w_ada.shape[1]
    tn = 1536
    return pl.pallas_call(
        _mod_kernel,
        grid=(n_out // tn,),
        in_specs=[
            pl.BlockSpec((rows, d), lambda j: (0, 0)),
            pl.BlockSpec((d, tn), lambda j: (0, j)),
            pl.BlockSpec((1, tn), lambda j: (0, j)),
        ],
        out_specs=pl.BlockSpec((rows, tn), lambda j: (0, j)),
        out_shape=jax.ShapeDtypeStruct((rows, n_out), F32),
        compiler_params=pltpu.CompilerParams(
            dimension_semantics=("arbitrary",), vmem_limit_bytes=VMEM_LIMIT),
        name="modulation",
    )(c_all, w_ada, b_ada.reshape(1, n_out))


def _inproj_kernel(x_ref, mod_ref, g_ref, w_ref, gq_ref, gk_ref, cos_ref, sin_ref, seg_ref,
                   o_ref):
    x = x_ref[0]
    sh = mod_ref[0, 0:1, :]
    sc = mod_ref[0, 1:2, :]
    h = _rms(x, g_ref[...]) * (1.0 + sc) + sh
    proj = jnp.dot(h.astype(BF16), w_ref[...], preferred_element_type=F32)

    tm = x.shape[0]
    cos = cos_ref[...]
    sin = sin_ref[...]
    seg = seg_ref[...]
    lane = lax.broadcasted_iota(jnp.int32, (tm, LANES), 1)
    first_half = (lane & 16) == 0
    low = lane < HEAD_DIM

    def norm_rope(t, g):
        ss = jnp.dot((t * t).astype(BF16), seg, preferred_element_type=F32)
        tn = t * lax.rsqrt(ss * (1.0 / HEAD_DIM) + EPS) * g
        rot = jnp.where(first_half, pltpu.roll(tn, LANES - 16, 1), pltpu.roll(tn, 16, 1))
        return tn * cos + rot * sin

    def put(blk, val):
        o_ref[0, :, blk * LANES:(blk + 1) * LANES] = val.astype(BF16)

    col = 0
    for j in range(GQA_Q_COLS // LANES):
        put(QA_BLK + j, norm_rope(proj[:, col:col + LANES], gq_ref[...]) * Q_SCALE)
        col += LANES
    kr = norm_rope(proj[:, col:col + LANES], gk_ref[...])
    col += LANES
    kr_sw = pltpu.roll(kr, HEAD_DIM, 1)
    put(KA_BLK + 0, jnp.where(low, kr, kr_sw))
    put(KA_BLK + 1, jnp.where(low, kr_sw, kr))
    va = proj[:, col:col + LANES]
    col += LANES
    va_sw = pltpu.roll(va, HEAD_DIM, 1)
    put(VA_BLK + 0, jnp.where(low, va, 1.0))
    put(VA_BLK + 1, jnp.where(low, va_sw, 1.0))
    for j in range(N_DIFF_HEADS):
        put(QD_BLK + j, proj[:, col:col + LANES] * Q_SCALE)
        col += LANES
    for j in range(N_DIFF_HEADS):
        put(KD_BLK + j, proj[:, col:col + LANES])
        col += LANES
    for j in range(N_DIFF_HEADS):
        put(VD_BLK + j, proj[:, col:col + LANES])
        col += LANES


def _in_projection(x, mod3, g_pre, w_in_bf, gq2, gk2, cos2, sin2, seg, tm=512):
    b, n, d = x.shape
    in_w = w_in_bf.shape[1]
    const = lambda bi, i: (0, 0)
    return pl.pallas_call(
        _inproj_kernel,
        grid=(b, n // tm),
        in_specs=[
            pl.BlockSpec((1, tm, d), lambda bi, i: (bi, i, 0)),
            pl.BlockSpec((1, 6, d), lambda bi, i: (bi, 0, 0)),
            pl.BlockSpec((1, d), const),
            pl.BlockSpec((d, in_w), const),
            pl.BlockSpec((1, LANES), const),
            pl.BlockSpec((1, LANES), const),
            pl.BlockSpec((tm, LANES), lambda bi, i: (i, 0)),
            pl.BlockSpec((tm, LANES), lambda bi, i: (i, 0)),
            pl.BlockSpec((LANES, LANES), const),
        ],
        out_specs=pl.BlockSpec((1, tm, ACT_COLS), lambda bi, i: (bi, i, 0)),
        out_shape=jax.ShapeDtypeStruct((b, n, ACT_COLS), BF16),
        compiler_params=pltpu.CompilerParams(
            dimension_semantics=("arbitrary", "arbitrary"), vmem_limit_bytes=VMEM_LIMIT),
        name="in_projection",
    )(x, mod3, g_pre, w_in_bf, gq2, gk2, cos2, sin2, seg)


def _flash_sweep(nc, tk, qz_ref, k_ref, v_ref, blocks, bufs, m_ref, acc_ref,
                 bias_fn=None, v_ones=None):
    assert nc % 2 == 0
    (s0_ref, mc0_ref), (s1_ref, mc1_ref) = bufs

    def scores(c, s_ref, mc_ref):
        start = pl.multiple_of(c * tk, tk)
        kc = k_ref[0, pl.ds(start, tk), :]
        bias = None if bias_fn is None else bias_fn(c)
        for rows in blocks:
            s = lax.dot_general(qz_ref[rows, :], kc, (((1,), (1,)), ((), ())),
                                preferred_element_type=F32)
            if bias is not None:
                s = s + bias
            s_ref[rows, :] = s
            mc_ref[rows, :] = jnp.broadcast_to(jnp.max(s, axis=-1, keepdims=True),
                                               (s.shape[0], LANES))

    def values(c, s_ref, mc_ref):
        start = pl.multiple_of(c * tk, tk)
        vc = v_ref[0, pl.ds(start, tk), :]
        if v_ones is not None:
            vc = jnp.concatenate([vc, v_ones], axis=1)
        for rows in blocks:
            m_prev = m_ref[rows, :]
            m_new = jnp.maximum(m_prev, mc_ref[rows, :])
            alpha = jnp.exp2(m_prev - m_new)
            p = jnp.concatenate(
                [jnp.exp2(s_ref[rows, j * LANES:(j + 1) * LANES] - m_new).astype(BF16)
                 for j in range(tk // LANES)], axis=1)
            pv = jnp.dot(p, vc, preferred_element_type=F32)
            reps = acc_ref.shape[1] // LANES
            scale = alpha if reps == 1 else jnp.concatenate([alpha] * reps, axis=1)
            acc_ref[rows, :] = scale * acc_ref[rows, :] + pv
            m_ref[rows, :] = m_new

    scores(0, s0_ref, mc0_ref)

    def pair(p, carry):
        c = 2 * p
        scores(c + 1, s1_ref, mc1_ref)
        values(c, s0_ref, mc0_ref)
        scores(c + 2, s0_ref, mc0_ref)
        values(c + 1, s1_ref, mc1_ref)
        return carry

    lax.fori_loop(0, nc // 2 - 1, pair, 0)
    scores(nc - 1, s1_ref, mc1_ref)
    values(nc - 2, s0_ref, mc0_ref)
    values(nc - 1, s1_ref, mc1_ref)


def _flash_scratch(mrows, tk, acc_cols):
    return [
        pltpu.VMEM((mrows, LANES), BF16),
        pltpu.VMEM((mrows, LANES), F32),
        pltpu.VMEM((mrows, acc_cols), F32),
        pltpu.VMEM((mrows, tk), F32),
        pltpu.VMEM((mrows, tk), F32),
        pltpu.VMEM((mrows, LANES), F32),
        pltpu.VMEM((mrows, LANES), F32),
    ]


def _gqa_kernel(q_ref, k_ref, v_ref, o_ref, qz_ref, m_ref, acc_ref, s0_ref, s1_ref,
                mc0_ref, mc1_ref, *, tq, tk):
    n = k_ref.shape[1]
    mrows = GQA_GROUP * tq
    lane = lax.broadcasted_iota(jnp.int32, (tq, LANES), 1)
    low = lane < HEAD_DIM
    for g in range(GQA_GROUP):
        slab = q_ref[0, :, (g // 2) * LANES:(g // 2 + 1) * LANES]
        keep = low if g % 2 == 0 else jnp.logical_not(low)
        qz_ref[g * tq:(g + 1) * tq, :] = jnp.where(keep, slab, jnp.zeros_like(slab))
    m_ref[...] = jnp.full(m_ref.shape, M_INIT, F32)
    acc_ref[...] = jnp.zeros(acc_ref.shape, F32)

    half = mrows // 2
    _flash_sweep(n // tk, tk, qz_ref, k_ref, v_ref, (slice(0, half), slice(half, mrows)),
                 ((s0_ref, mc0_ref), (s1_ref, mc1_ref)), m_ref, acc_ref)

    outs = []
    for g in range(GQA_GROUP):
        a = acc_ref[g * tq:(g + 1) * tq, :]
        outs.append(a / pltpu.roll(a, HEAD_DIM, 1))
    for j in range(GQA_GROUP // 2):
        pair = jnp.where(low, outs[2 * j], pltpu.roll(outs[2 * j + 1], HEAD_DIM, 1))
        o_ref[0, :, j * LANES:(j + 1) * LANES] = pair.astype(o_ref.dtype)


def _gqa_attention(act, tq=256, tk=512):
    b, n, _ = act.shape
    width = GQA_GROUP * HEAD_DIM
    mrows = GQA_GROUP * tq
    return pl.pallas_call(
        functools.partial(_gqa_kernel, tq=tq, tk=tk),
        grid=(b, N_GQA_KV, n // tq),
        in_specs=[
            pl.BlockSpec((1, tq, width), lambda bi, h, i: (bi, i, h)),
            pl.BlockSpec((1, n, LANES), lambda bi, h, i: (bi, 0, KA_BLK + h)),
            pl.BlockSpec((1, n, LANES), lambda bi, h, i: (bi, 0, VA_BLK + h)),
        ],
        out_specs=pl.BlockSpec((1, tq, width), lambda bi, h, i: (bi, i, h)),
        out_shape=jax.ShapeDtypeStruct((b, n, GQA_Q_COLS), BF16),
        scratch_shapes=_flash_scratch(mrows, tk, LANES),
        compiler_params=pltpu.CompilerParams(
            dimension_semantics=("arbitrary", "arbitrary", "arbitrary"),
            vmem_limit_bytes=VMEM_LIMIT),
        name="gqa_attention",
    )(act, act, act)


def _rel_bucket(rel):
    half = NUM_BUCKETS // 2
    max_exact = half // 2
    n = jnp.minimum(jnp.abs(rel), MAX_DISTANCE)
    n2 = n * n
    large = jnp.full(rel.shape, max_exact, jnp.int32)
    for k in range(1, half - max_exact):
        large = large + (n2 >= (max_exact * max_exact) * (2 ** k)).astype(jnp.int32)
    return jnp.where(rel > 0, half, 0) + jnp.where(n < max_exact, n, large)


N_BIAS_TILES = 5


def _diff_kernel(rb_ref, lam_ref, gs_ref, q_ref, k_ref, v_ref, o_ref,
                 qz_ref, m_ref, acc_ref, s0_ref, s1_ref, mc0_ref, mc1_ref, bias_ref,
                 *, t, lam_init):
    h = pl.program_id(0)
    bi = pl.program_id(1)
    i = pl.program_id(2)
    n = k_ref.shape[1]
    nc = n // t
    reach = N_BIAS_TILES // 2

    @pl.when(jnp.logical_and(bi == 0, i == 0))
    def _():
        def fill(r, carry):
            r0 = pl.multiple_of(r * 8, 8)
            row = r0 + lax.broadcasted_iota(jnp.int32, (8, t), 0)
            col = lax.broadcasted_iota(jnp.int32, (8, t), 1)
            for d in range(-reach, reach + 1):
                bucket = _rel_bucket(d * t + col - row)
                val = jnp.zeros((8, t), F32)
                for bkt in range(NUM_BUCKETS):
                    val = jnp.where(bucket == bkt, rb_ref[bkt, h] * LOG2E, val)
                bias_ref[d + reach, pl.ds(r0, 8), :] = val
            return carry
        lax.fori_loop(0, t // 8, fill, 0)

    lane = lax.broadcasted_iota(jnp.int32, (t, LANES), 1)
    low = lane < HEAD_DIM
    q = q_ref[0]
    zero = jnp.zeros_like(q)
    qz_ref[0:t, :] = jnp.where(low, q, zero)
    qz_ref[t:2 * t, :] = jnp.where(low, zero, q)
    m_ref[...] = jnp.full(m_ref.shape, M_INIT, F32)
    acc_ref[...] = jnp.zeros(acc_ref.shape, F32)

    def bias_fn(c):
        return bias_ref[jnp.clip(c - i, -reach, reach) + reach]

    _flash_sweep(nc, t, qz_ref, k_ref, v_ref, (slice(0, t), slice(t, 2 * t)),
                 ((s0_ref, mc0_ref), (s1_ref, mc1_ref)), m_ref, acc_ref,
                 bias_fn=bias_fn, v_ones=jnp.ones((t, LANES), BF16))

    lv = lam_ref[...]
    lam = (jnp.exp(jnp.sum(lv[0:1, :] * lv[1:2, :], axis=-1, keepdims=True))
           - jnp.exp(jnp.sum(lv[2:3, :] * lv[3:4, :], axis=-1, keepdims=True)) + lam_init)
    o1 = acc_ref[0:t, 0:LANES] / acc_ref[0:t, LANES:2 * LANES]
    o2 = acc_ref[t:2 * t, 0:LANES] / acc_ref[t:2 * t, LANES:2 * LANES]
    o = o1 - lam * o2
    o_ref[0] = (_rms(o, gs_ref[...]) * (1.0 - lam_init)).astype(o_ref.dtype)


def _diff_attention(act, rel_bias, lam_rows, g_subln, lam_init, t=512):
    b, n, _ = act.shape
    return pl.pallas_call(
        functools.partial(_diff_kernel, t=t, lam_init=lam_init),
        grid=(N_DIFF_HEADS, b, n // t),
        in_specs=[
            pl.BlockSpec(memory_space=pltpu.SMEM),
            pl.BlockSpec((MOD_ROWS, LANES), lambda h, bi, i: (0, 0)),
            pl.BlockSpec((1, LANES), lambda h, bi, i: (0, 0)),
            pl.BlockSpec((1, t, LANES), lambda h, bi, i: (bi, i, QD_BLK + h)),
            pl.BlockSpec((1, n, LANES), lambda h, bi, i: (bi, 0, KD_BLK + h)),
            pl.BlockSpec((1, n, LANES), lambda h, bi, i: (bi, 0, VD_BLK + h)),
        ],
        out_specs=pl.BlockSpec((1, t, LANES), lambda h, bi, i: (bi, i, h)),
        out_shape=jax.ShapeDtypeStruct((b, n, DIFF_COLS), BF16),
        scratch_shapes=_flash_scratch(2 * t, t, 2 * LANES)
        + [pltpu.VMEM((N_BIAS_TILES, t, t), F32)],
        compiler_params=pltpu.CompilerParams(
            dimension_semantics=("arbitrary", "arbitrary", "arbitrary"),
            vmem_limit_bytes=VMEM_LIMIT),
        name="diff_attention",
    )(rel_bias, lam_rows, g_subln, act, act, act)


def _ffn_kernel(x_ref, oa_ref, od_ref, mod_ref, wout_ref, gpm_ref, gpf_ref, wgu_ref, wdn_ref,
                gpo_ref, y_ref):
    x = x_ref[0]
    gt1 = mod_ref[0, 2:3, :]
    sh2 = mod_ref[0, 3:4, :]
    sc2 = mod_ref[0, 4:5, :]
    gt2 = mod_ref[0, 5:6, :]
    wa = oa_ref.shape[2]
    mix = jnp.dot(oa_ref[0], wout_ref[0:wa, :], preferred_element_type=F32)
    mix += jnp.dot(od_ref[0], wout_ref[wa:, :], preferred_element_type=F32)
    x1 = x + gt1 * _rms(mix, gpm_ref[...])
    h = _rms(x1, gpf_ref[...]) * (1.0 + sc2) + sh2
    gu = jnp.dot(h.astype(BF16), wgu_ref[...], preferred_element_type=F32)
    d_ff = wdn_ref.shape[0]
    gate = gu[:, :d_ff]
    up = gu[:, d_ff:]
    act = gate * (1.0 / (1.0 + jnp.exp(-gate))) * up
    f = jnp.dot(act.astype(BF16), wdn_ref[...], preferred_element_type=F32)
    y_ref[0] = x1 + gt2 * _rms(f, gpo_ref[...])


def _out_ffn(x, out_a, out_d, mod3, w_out_bf, g_post_mix, g_pre_ffn, w_gu_bf, w_down_bf,
             g_post_ffn, tm=256):
    b, n, d = x.shape
    const = lambda bi, i: (0, 0)
    once = pl.Buffered(1)
    return pl.pallas_call(
        _ffn_kernel,
        grid=(b, n // tm),
        in_specs=[
            pl.BlockSpec((1, tm, d), lambda bi, i: (bi, i, 0)),
            pl.BlockSpec((1, tm, out_a.shape[2]), lambda bi, i: (bi, i, 0)),
            pl.BlockSpec((1, tm, out_d.shape[2]), lambda bi, i: (bi, i, 0)),
            pl.BlockSpec((1, 6, d), lambda bi, i: (bi, 0, 0)),
            pl.BlockSpec(w_out_bf.shape, const, pipeline_mode=once),
            pl.BlockSpec((1, d), const),
            pl.BlockSpec((1, d), const),
            pl.BlockSpec(w_gu_bf.shape, const, pipeline_mode=once),
            pl.BlockSpec(w_down_bf.shape, const, pipeline_mode=once),
            pl.BlockSpec((1, d), const),
        ],
        out_specs=pl.BlockSpec((1, tm, d), lambda bi, i: (bi, i, 0)),
        out_shape=jax.ShapeDtypeStruct((b, n, d), F32),
        compiler_params=pltpu.CompilerParams(
            dimension_semantics=("arbitrary", "arbitrary"), vmem_limit_bytes=VMEM_LIMIT),
        name="out_ffn",
    )(x, out_a, out_d, mod3, w_out_bf, g_post_mix, g_pre_ffn, w_gu_bf, w_down_bf, g_post_ffn)


def _rope_tables(n):
    rows = n // GRID_W
    row = jnp.repeat(jnp.arange(rows), GRID_W).astype(F32)
    col = jnp.tile(jnp.arange(GRID_W), rows).astype(F32)
    half = HEAD_DIM // 2
    inv = ROPE_THETA ** (-jnp.arange(0, half, 2, dtype=F32) / half)
    ang_r = row[:, None] * inv[None, :]
    ang_c = col[:, None] * inv[None, :]
    ang = jnp.concatenate([ang_r, ang_r, ang_c, ang_c], axis=-1)
    sign = jnp.tile(jnp.concatenate([-jnp.ones((16,), F32), jnp.ones((16,), F32)]), 2)
    cos = jnp.cos(ang)
    sin = jnp.sin(ang) * sign[None, :]
    return jnp.tile(cos, (1, 2)), jnp.tile(sin, (1, 2))


def kernel(x_prompt, x_sample, c_prompt, c_sample, rel_bias, w_ada, b_ada, g_pre_mix, w_in,
           g_q, g_k, lam_q1, lam_k1, lam_q2, lam_k2, g_subln, w_out, g_post_mix, g_pre_ffn,
           w_gu, w_down, g_post_ffn):
    depth = w_ada.shape[0]
    d = x_prompt.shape[-1]
    xs = [x_prompt, x_sample]
    cs = [c_prompt, c_sample]
    n_c = sum(c.shape[0] for c in cs)
    assert n_c <= MOD_ROWS
    c_all = jnp.concatenate(cs + [jnp.zeros((MOD_ROWS - n_c, d), F32)], axis=0)
    seg = (jnp.arange(LANES)[:, None] // HEAD_DIM
           == jnp.arange(LANES)[None, :] // HEAD_DIM).astype(BF16)
    tables = [_rope_tables(x.shape[1]) for x in xs]

    for l in range(depth):
        lam_init = 0.8 - 0.6 * math.exp(-0.3 * l)
        mod = _modulation(c_all, w_ada[l], b_ada[l]).reshape(MOD_ROWS, 6, d)
        w_in_bf = w_in[l].astype(BF16)
        w_out_bf = w_out[l].astype(BF16)
        w_gu_bf = w_gu[l].astype(BF16)
        w_down_bf = w_down[l].astype(BF16)
        gq2 = jnp.tile(g_q[l], LANES // HEAD_DIM).reshape(1, LANES)
        gk2 = jnp.tile(g_k[l], LANES // HEAD_DIM).reshape(1, LANES)
        lam_rows = jnp.zeros((MOD_ROWS, LANES), F32).at[0:4, 0:HEAD_DIM].set(
            jnp.stack([lam_q1[l], lam_k1[l], lam_q2[l], lam_k2[l]]))
        new_xs = []
        row0 = 0
        for x, (cos2, sin2) in zip(xs, tables):
            mod3 = mod[row0:row0 + x.shape[0]]
            row0 += x.shape[0]
            act = _in_projection(x, mod3, g_pre_mix[l].reshape(1, d), w_in_bf, gq2, gk2,
                                 cos2, sin2, seg)
            out_a = _gqa_attention(act)
            out_d = _diff_attention(act, rel_bias, lam_rows, g_subln[l].reshape(1, LANES),
                                    lam_init)
            new_xs.append(_out_ffn(x, out_a, out_d, mod3, w_out_bf,
                                   g_post_mix[l].reshape(1, d), g_pre_ffn[l].reshape(1, d),
                                   w_gu_bf, w_down_bf, g_post_ffn[l].reshape(1, d)))
        xs = new_xs
    return tuple(xs)
```

```python
import functools
import math

import jax
import jax.numpy as jnp
from jax import lax
from jax.experimental import pallas as pl
from jax.experimental.pallas import tpu as pltpu

F32 = jnp.float32
BF16 = jnp.bfloat16

HEAD_DIM = 64
N_GQA_HEADS = 8
N_GQA_KV = 2
GQA_GROUP = N_GQA_HEADS // N_GQA_KV
N_DIFF_HEADS = 4
GRID_W = 64
NUM_BUCKETS = 32
MAX_DISTANCE = 128
ROPE_THETA = 10000.0
EPS = 1e-6
ATTN_SCALE = 1.0 / math.sqrt(HEAD_DIM)
LOG2E = 1.4426950408889634
Q_SCALE = ATTN_SCALE * LOG2E

LANES = 128
GQA_Q_COLS = N_GQA_HEADS * HEAD_DIM
GQA_KV_COLS = N_GQA_KV * HEAD_DIM
DIFF_COLS = N_DIFF_HEADS * 2 * HEAD_DIM

QA_BLK = 0
KA_BLK = QA_BLK + GQA_Q_COLS // LANES
VA_BLK = KA_BLK + N_GQA_KV
QD_BLK = VA_BLK + N_GQA_KV
KD_BLK = QD_BLK + N_DIFF_HEADS
VD_BLK = KD_BLK + N_DIFF_HEADS
ACT_COLS = (VD_BLK + N_DIFF_HEADS) * LANES

MOD_ROWS = 8
M_INIT = -0.5 * float(jnp.finfo(jnp.float32).max)

VMEM_LIMIT = 56 * 1024 * 1024
FLASH_UNROLL = 4
STRIP_ROWS = 32
BLOCK_ROWS = 512


def _rms(x, g):
    return x * lax.rsqrt(jnp.mean(x * x, axis=-1, keepdims=True) + EPS) * g


def _mod_kernel(c_ref, w_ref, b_ref, o_ref):
    c = c_ref[...]
    a = c * (1.0 / (1.0 + jnp.exp(-c)))
    a_hi = a.astype(BF16)
    a_lo = (a - a_hi.astype(F32)).astype(BF16)
    w = w_ref[...]
    w_hi = w.astype(BF16)
    w_lo = (w - w_hi.astype(F32)).astype(BF16)
    acc = jnp.dot(a_hi, w_hi, preferred_element_type=F32)
    acc += jnp.dot(a_lo, w_hi, preferred_element_type=F32)
    acc += jnp.dot(a_hi, w_lo, preferred_element_type=F32)
    o_ref[...] = acc + b_ref[...]


def _modulation(c_all, w_ada, b_ada):
    rows, d = c_all.shape
    n_out = w_ada.shape[1]
    tn = 1536
    return pl.pallas_call(
        _mod_kernel,
        grid=(n_out // tn,),
        in_specs=[
            pl.BlockSpec((rows, d), lambda j: (0, 0)),
            pl.BlockSpec((d, tn), lambda j: (0, j)),
            pl.BlockSpec((1, tn), lambda j: (0, j)),
        ],
        out_specs=pl.BlockSpec((rows, tn), lambda j: (0, j)),
        out_shape=jax.ShapeDtypeStruct((rows, n_out), F32),
        compiler_params=pltpu.CompilerParams(
            dimension_semantics=("arbitrary",), vmem_limit_bytes=VMEM_LIMIT),
        name="modulation",
    )(c_all, w_ada, b_ada.reshape(1, n_out))


def _inproj_kernel(x_ref, mod_ref, g_ref, w_ref, gq_ref, gk_ref, cos_ref, sin_ref, seg_ref,
                   o_ref):
    x = x_ref[0]
    sh = mod_ref[0, 0:1, :]
    sc = mod_ref[0, 1:2, :]
    h = _rms(x, g_ref[...]) * (1.0 + sc) + sh
    proj = jnp.dot(h.astype(BF16), w_ref[...], preferred_element_type=F32)

    tm = x.shape[0]
    cos = cos_ref[...]
    sin = sin_ref[...]
    seg = seg_ref[...]
    lane = lax.broadcasted_iota(jnp.int32, (tm, LANES), 1)
    first_half = (lane & 16) == 0
    low = lane < HEAD_DIM

    def norm_rope(t, g):
        ss = jnp.dot((t * t).astype(BF16), seg, preferred_element_type=F32)
        tn = t * lax.rsqrt(ss * (1.0 / HEAD_DIM) + EPS) * g
        rot = jnp.where(first_half, pltpu.roll(tn, LANES - 16, 1), pltpu.roll(tn, 16, 1))
        return tn * cos + rot * sin

    def put(blk, val):
        o_ref[0, :, blk * LANES:(blk + 1) * LANES] = val.astype(BF16)

    col = 0
    for j in range(GQA_Q_COLS // LANES):
        put(QA_BLK + j, norm_rope(proj[:, col:col + LANES], gq_ref[...]) * Q_SCALE)
        col += LANES
    kr = norm_rope(proj[:, col:col + LANES], gk_ref[...])
    col += LANES
    kr_sw = pltpu.roll(kr, HEAD_DIM, 1)
    put(KA_BLK + 0, jnp.where(low, kr, kr_sw))
    put(KA_BLK + 1, jnp.where(low, kr_sw, kr))
    va = proj[:, col:col + LANES]
    col += LANES
    va_sw = pltpu.roll(va, HEAD_DIM, 1)
    put(VA_BLK + 0, jnp.where(low, va, 1.0))
    put(VA_BLK + 1, jnp.where(low, va_sw, 1.0))
    for j in range(N_DIFF_HEADS):
        put(QD_BLK + j, proj[:, col:col + LANES] * Q_SCALE)
        col += LANES
    for j in range(N_DIFF_HEADS):
        put(KD_BLK + j, proj[:, col:col + LANES])
        col += LANES
    for j in range(N_DIFF_HEADS):
        put(VD_BLK + j, proj[:, col:col + LANES])
        col += LANES


def _in_projection(x, mod3, g_pre, w_in_bf, gq2, gk2, cos2, sin2, seg, tm=512):
    b, n, d = x.shape
    in_w = w_in_bf.shape[1]
    const = lambda bi, i: (0, 0)
    return pl.pallas_call(
        _inproj_kernel,
        grid=(b, n // tm),
        in_specs=[
            pl.BlockSpec((1, tm, d), lambda bi, i: (bi, i, 0)),
            pl.BlockSpec((1, 6, d), lambda bi, i: (bi, 0, 0)),
            pl.BlockSpec((1, d), const),
            pl.BlockSpec((d, in_w), const),
            pl.BlockSpec((1, LANES), const),
            pl.BlockSpec((1, LANES), const),
            pl.BlockSpec((tm, LANES), lambda bi, i: (i, 0)),
            pl.BlockSpec((tm, LANES), lambda bi, i: (i, 0)),
            pl.BlockSpec((LANES, LANES), const),
        ],
        out_specs=pl.BlockSpec((1, tm, ACT_COLS), lambda bi, i: (bi, i, 0)),
        out_shape=jax.ShapeDtypeStruct((b, n, ACT_COLS), BF16),
        compiler_params=pltpu.CompilerParams(
            dimension_semantics=("arbitrary", "arbitrary"), vmem_limit_bytes=VMEM_LIMIT),
        name="in_projection",
    )(x, mod3, g_pre, w_in_bf, gq2, gk2, cos2, sin2, seg)


def _flash_sweep(nc, tk, qz_ref, k_ref, v_ref, blocks, bufs, m_ref, acc_ref, *, v_ones=None,
                 n_tail=0, main_chunk=None, main_shift=None, tail_chunk=None, tail_bias=None):
    n_main = nc - n_tail
    unroll = min(FLASH_UNROLL, n_main) if n_main else 0
    assert n_tail % 2 == 0 and (n_main == 0 or (unroll % 2 == 0 and n_main % unroll == 0))

    def chunk_of(kind, idx):
        if kind == "tail":
            return tail_chunk(idx), None, tail_bias
        c = idx if main_chunk is None else main_chunk(idx)
        return c, (None if main_shift is None else main_shift(c)), None

    def scores(kind, idx, s_ref, mc_ref):
        c, shift, bias = chunk_of(kind, idx)
        start = pl.multiple_of(c * tk, tk)
        kc = k_ref[0, pl.ds(start, tk), :]
        for rows in blocks:
            s = lax.dot_general(qz_ref[rows, :], kc, (((1,), (1,)), ((), ())),
                                preferred_element_type=F32)
            if bias is not None:
                s = s + bias(c)
            s_ref[rows, :] = s
            mc = jnp.max(s, axis=-1, keepdims=True)
            if shift is not None:
                mc = mc + shift
            mc_ref[rows, :] = jnp.broadcast_to(mc, (s.shape[0], LANES))

    def values(kind, idx, s_ref, mc_ref):
        c, shift, _ = chunk_of(kind, idx)
        start = pl.multiple_of(c * tk, tk)
        vc = v_ref[0, pl.ds(start, tk), :]
        if v_ones is not None:
            vc = jnp.concatenate([vc, v_ones], axis=1)
        reps = acc_ref.shape[1] // LANES
        for rows in blocks:
            strips = []
            for r0 in range(rows.start, rows.stop, STRIP_ROWS):
                rs = slice(r0, r0 + STRIP_ROWS)
                m_prev = m_ref[rs, :]
                m_new = jnp.maximum(m_prev, mc_ref[rs, :])
                alpha = jnp.exp2(m_prev - m_new)
                m_ref[rs, :] = m_new
                for j in range(reps):
                    cols = slice(j * LANES, (j + 1) * LANES)
                    acc_ref[rs, cols] = alpha * acc_ref[rs, cols]
                m_sub = m_new if shift is None else m_new - shift
                strips.append(jnp.concatenate(
                    [jnp.exp2(s_ref[rs, j * LANES:(j + 1) * LANES] - m_sub).astype(BF16)
                     for j in range(tk // LANES)], axis=1))
            p = jnp.concatenate(strips, axis=0)
            acc_ref[rows, :] += jnp.dot(p, vc, preferred_element_type=F32)

    def group(kind, base, count, nxt):
        for u in range(count):
            if u + 1 < count:
                scores(kind, base + u + 1, *bufs[(u + 1) % 2])
            elif nxt is not None:
                scores(*nxt, *bufs[(u + 1) % 2])
            values(kind, base + u, *bufs[u % 2])

    tail_start = ("tail", 0) if n_tail else None
    scores(*(("main", 0) if n_main else tail_start), *bufs[0])
    if n_main:
        n_groups = n_main // unroll

        def body(g, carry):
            group("main", g * unroll, unroll, ("main", (g + 1) * unroll))
            return carry

        lax.fori_loop(0, n_groups - 1, body, 0)
        group("main", n_main - unroll, unroll, tail_start)
    if n_tail:
        group("tail", 0, n_tail, None)


def _flash_scratch(mrows, tk, acc_cols):
    return [
        pltpu.VMEM((mrows, LANES), BF16),
        pltpu.VMEM((mrows, LANES), F32),
        pltpu.VMEM((mrows, acc_cols), F32),
        pltpu.VMEM((mrows, tk), F32),
        pltpu.VMEM((mrows, tk), F32),
        pltpu.VMEM((mrows, LANES), F32),
        pltpu.VMEM((mrows, LANES), F32),
    ]


def _gqa_kernel(q_ref, k_ref, v_ref, o_ref, qz_ref, m_ref, acc_ref, s0_ref, s1_ref,
                mc0_ref, mc1_ref, *, tq, tk):
    n = k_ref.shape[1]
    mrows = GQA_GROUP * tq
    lane = lax.broadcasted_iota(jnp.int32, (tq, LANES), 1)
    low = lane < HEAD_DIM
    for g in range(GQA_GROUP):
        slab = q_ref[0, :, (g // 2) * LANES:(g // 2 + 1) * LANES]
        keep = low if g % 2 == 0 else jnp.logical_not(low)
        qz_ref[g * tq:(g + 1) * tq, :] = jnp.where(keep, slab, jnp.zeros_like(slab))
    m_ref[...] = jnp.full(m_ref.shape, M_INIT, F32)
    acc_ref[...] = jnp.zeros(acc_ref.shape, F32)

    blocks = tuple(slice(r, r + BLOCK_ROWS) for r in range(0, mrows, BLOCK_ROWS))
    _flash_sweep(n // tk, tk, qz_ref, k_ref, v_ref, blocks,
                 ((s0_ref, mc0_ref), (s1_ref, mc1_ref)), m_ref, acc_ref)

    outs = []
    for g in range(GQA_GROUP):
        a = acc_ref[g * tq:(g + 1) * tq, :]
        outs.append(a / pltpu.roll(a, HEAD_DIM, 1))
    for j in range(GQA_GROUP // 2):
        pair = jnp.where(low, outs[2 * j], pltpu.roll(outs[2 * j + 1], HEAD_DIM, 1))
        o_ref[0, :, j * LANES:(j + 1) * LANES] = pair.astype(o_ref.dtype)


def _gqa_attention(act, tq=512, tk=512):
    b, n, _ = act.shape
    width = GQA_GROUP * HEAD_DIM
    mrows = GQA_GROUP * tq
    return pl.pallas_call(
        functools.partial(_gqa_kernel, tq=tq, tk=tk),
        grid=(b, N_GQA_KV, n // tq),
        in_specs=[
            pl.BlockSpec((1, tq, width), lambda bi, h, i: (bi, i, h)),
            pl.BlockSpec((1, n, LANES), lambda bi, h, i: (bi, 0, KA_BLK + h)),
            pl.BlockSpec((1, n, LANES), lambda bi, h, i: (bi, 0, VA_BLK + h)),
        ],
        out_specs=pl.BlockSpec((1, tq, width), lambda bi, h, i: (bi, i, h)),
        out_shape=jax.ShapeDtypeStruct((b, n, GQA_Q_COLS), BF16),
        scratch_shapes=_flash_scratch(mrows, tk, LANES),
        compiler_params=pltpu.CompilerParams(
            dimension_semantics=("arbitrary", "arbitrary", "arbitrary"),
            vmem_limit_bytes=VMEM_LIMIT),
        name="gqa_attention",
    )(act, act, act)


def _rel_bucket(rel):
    half = NUM_BUCKETS // 2
    max_exact = half // 2
    n = jnp.minimum(jnp.abs(rel), MAX_DISTANCE)
    n2 = n * n
    large = jnp.full(rel.shape, max_exact, jnp.int32)
    for k in range(1, half - max_exact):
        large = large + (n2 >= (max_exact * max_exact) * (2 ** k)).astype(jnp.int32)
    return jnp.where(rel > 0, half, 0) + jnp.where(n < max_exact, n, large)


N_BIAS_TILES = 5
NEAR_CHUNKS = 4


def _diff_kernel(rb_ref, lam_ref, gs_ref, q_ref, k_ref, v_ref, o_ref,
                 qz_ref, m_ref, acc_ref, s0_ref, s1_ref, mc0_ref, mc1_ref, bias_ref,
                 *, t, lam_init):
    h = pl.program_id(0)
    bi = pl.program_id(1)
    i = pl.program_id(2)
    n = k_ref.shape[1]
    nc = n // t
    reach = N_BIAS_TILES // 2

    @pl.when(jnp.logical_and(bi == 0, i == 0))
    def _():
        def fill(r, carry):
            r0 = pl.multiple_of(r * 8, 8)
            row = r0 + lax.broadcasted_iota(jnp.int32, (8, t), 0)
            col = lax.broadcasted_iota(jnp.int32, (8, t), 1)
            for d in range(-reach, reach + 1):
                bucket = _rel_bucket(d * t + col - row)
                val = jnp.zeros((8, t), F32)
                for bkt in range(NUM_BUCKETS):
                    val = jnp.where(bucket == bkt, rb_ref[bkt, h] * LOG2E, val)
                bias_ref[d + reach, pl.ds(r0, 8), :] = val
            return carry
        lax.fori_loop(0, t // 8, fill, 0)

    lane = lax.broadcasted_iota(jnp.int32, (t, LANES), 1)
    low = lane < HEAD_DIM
    q = q_ref[0]
    zero = jnp.zeros_like(q)
    qz_ref[0:t, :] = jnp.where(low, q, zero)
    qz_ref[t:2 * t, :] = jnp.where(low, zero, q)
    m_ref[...] = jnp.full(m_ref.shape, M_INIT, F32)
    acc_ref[...] = jnp.zeros(acc_ref.shape, F32)

    n_near = min(NEAR_CHUNKS, nc)
    near0 = jnp.clip(i - 1, 0, nc - n_near)
    shift_before = rb_ref[NUM_BUCKETS // 2 - 1, h] * LOG2E
    shift_after = rb_ref[NUM_BUCKETS - 1, h] * LOG2E

    _flash_sweep(
        nc, t, qz_ref, k_ref, v_ref, (slice(0, t), slice(t, 2 * t)),
        ((s0_ref, mc0_ref), (s1_ref, mc1_ref)), m_ref, acc_ref,
        v_ones=jnp.ones((t, LANES), BF16), n_tail=n_near,
        main_chunk=lambda pos: pos + jnp.where(pos >= near0, n_near, 0),
        main_shift=lambda c: jnp.where(c < near0, shift_before, shift_after),
        tail_chunk=lambda u: near0 + u,
        tail_bias=lambda c: bias_ref[jnp.clip(c - i, -reach, reach) + reach])

    lv = lam_ref[...]
    lam = (jnp.exp(jnp.sum(lv[0:1, :] * lv[1:2, :], axis=-1, keepdims=True))
           - jnp.exp(jnp.sum(lv[2:3, :] * lv[3:4, :], axis=-1, keepdims=True)) + lam_init)
    o1 = acc_ref[0:t, 0:LANES] / acc_ref[0:t, LANES:2 * LANES]
    o2 = acc_ref[t:2 * t, 0:LANES] / acc_ref[t:2 * t, LANES:2 * LANES]
    o = o1 - lam * o2
    o_ref[0] = (_rms(o, gs_ref[...]) * (1.0 - lam_init)).astype(o_ref.dtype)


def _diff_attention(act, rel_bias, lam_rows, g_subln, lam_init, t=512):
    b, n, _ = act.shape
    return pl.pallas_call(
        functools.partial(_diff_kernel, t=t, lam_init=lam_init),
        grid=(N_DIFF_HEADS, b, n // t),
        in_specs=[
            pl.BlockSpec(memory_space=pltpu.SMEM),
            pl.BlockSpec((MOD_ROWS, LANES), lambda h, bi, i: (0, 0)),
            pl.BlockSpec((1, LANES), lambda h, bi, i: (0, 0)),
            pl.BlockSpec((1, t, LANES), lambda h, bi, i: (bi, i, QD_BLK + h)),
            pl.BlockSpec((1, n, LANES), lambda h, bi, i: (bi, 0, KD_BLK + h)),
            pl.BlockSpec((1, n, LANES), lambda h, bi, i: (bi, 0, VD_BLK + h)),
        ],
        out_specs=pl.BlockSpec((1, t, LANES), lambda h, bi, i: (bi, i, h)),
        out_shape=jax.ShapeDtypeStruct((b, n, DIFF_COLS), BF16),
        scratch_shapes=_flash_scratch(2 * t, t, 2 * LANES)
        + [pltpu.VMEM((N_BIAS_TILES, t, t), F32)],
        compiler_params=pltpu.CompilerParams(
            dimension_semantics=("arbitrary", "arbitrary", "arbitrary"),
            vmem_limit_bytes=VMEM_LIMIT),
        name="diff_attention",
    )(rel_bias, lam_rows, g_subln, act, act, act)


def _ffn_kernel(x_ref, oa_ref, od_ref, mod_ref, wout_ref, gpm_ref, gpf_ref, wgu_ref, wdn_ref,
                gpo_ref, y_ref):
    x = x_ref[0]
    gt1 = mod_ref[0, 2:3, :]
    sh2 = mod_ref[0, 3:4, :]
    sc2 = mod_ref[0, 4:5, :]
    gt2 = mod_ref[0, 5:6, :]
    wa = oa_ref.shape[2]
    mix = jnp.dot(oa_ref[0], wout_ref[0:wa, :], preferred_element_type=F32)
    mix += jnp.dot(od_ref[0], wout_ref[wa:, :], preferred_element_type=F32)
    x1 = x + gt1 * _rms(mix, gpm_ref[...])
    h = _rms(x1, gpf_ref[...]) * (1.0 + sc2) + sh2
    gu = jnp.dot(h.astype(BF16), wgu_ref[...], preferred_element_type=F32)
    d_ff = wdn_ref.shape[0]
    gate = gu[:, :d_ff]
    up = gu[:, d_ff:]
    act = gate * (1.0 / (1.0 + jnp.exp(-gate))) * up
    f = jnp.dot(act.astype(BF16), wdn_ref[...], preferred_element_type=F32)
    y_ref[0] = x1 + gt2 * _rms(f, gpo_ref[...])


def _out_ffn(x, out_a, out_d, mod3, w_out_bf, g_post_mix, g_pre_ffn, w_gu_bf, w_down_bf,
             g_post_ffn, tm=256):
    b, n, d = x.shape
    const = lambda bi, i: (0, 0)
    once = pl.Buffered(1)
    return pl.pallas_call(
        _ffn_kernel,
        grid=(b, n // tm),
        in_specs=[
            pl.BlockSpec((1, tm, d), lambda bi, i: (bi, i, 0)),
            pl.BlockSpec((1, tm, out_a.shape[2]), lambda bi, i: (bi, i, 0)),
            pl.BlockSpec((1, tm, out_d.shape[2]), lambda bi, i: (bi, i, 0)),
            pl.BlockSpec((1, 6, d), lambda bi, i: (bi, 0, 0)),
            pl.BlockSpec(w_out_bf.shape, const, pipeline_mode=once),
            pl.BlockSpec((1, d), const),
            pl.BlockSpec((1, d), const),
            pl.BlockSpec(w_gu_bf.shape, const, pipeline_mode=once),
            pl.BlockSpec(w_down_bf.shape, const, pipeline_mode=once),
            pl.BlockSpec((1, d), const),
        ],
        out_specs=pl.BlockSpec((1, tm, d), lambda bi, i: (bi, i, 0)),
        out_shape=jax.ShapeDtypeStruct((b, n, d), F32),
        compiler_params=pltpu.CompilerParams(
            dimension_semantics=("arbitrary", "arbitrary"), vmem_limit_bytes=VMEM_LIMIT),
        name="out_ffn",
    )(x, out_a, out_d, mod3, w_out_bf, g_post_mix, g_pre_ffn, w_gu_bf, w_down_bf, g_post_ffn)


def _rope_tables(n):
    rows = n // GRID_W
    row = jnp.repeat(jnp.arange(rows), GRID_W).astype(F32)
    col = jnp.tile(jnp.arange(GRID_W), rows).astype(F32)
    half = HEAD_DIM // 2
    inv = ROPE_THETA ** (-jnp.arange(0, half, 2, dtype=F32) / half)
    ang_r = row[:, None] * inv[None, :]
    ang_c = col[:, None] * inv[None, :]
    ang = jnp.concatenate([ang_r, ang_r, ang_c, ang_c], axis=-1)
    sign = jnp.tile(jnp.concatenate([-jnp.ones((16,), F32), jnp.ones((16,), F32)]), 2)
    cos = jnp.cos(ang)
    sin = jnp.sin(ang) * sign[None, :]
    return jnp.tile(cos, (1, 2)), jnp.tile(sin, (1, 2))


def kernel(x_prompt, x_sample, c_prompt, c_sample, rel_bias, w_ada, b_ada, g_pre_mix, w_in,
           g_q, g_k, lam_q1, lam_k1, lam_q2, lam_k2, g_subln, w_out, g_post_mix, g_pre_ffn,
           w_gu, w_down, g_post_ffn):
    depth = w_ada.shape[0]
    d = x_prompt.shape[-1]
    xs = [x_prompt, x_sample]
    cs = [c_prompt, c_sample]
    n_c = sum(c.shape[0] for c in cs)
    assert n_c <= MOD_ROWS
    c_all = jnp.concatenate(cs + [jnp.zeros((MOD_ROWS - n_c, d), F32)], axis=0)
    seg = (jnp.arange(LANES)[:, None] // HEAD_DIM
           == jnp.arange(LANES)[None, :] // HEAD_DIM).astype(BF16)
    tables = [_rope_tables(x.shape[1]) for x in xs]

    for l in range(depth):
        lam_init = 0.8 - 0.6 * math.exp(-0.3 * l)
        mod = _modulation(c_all, w_ada[l], b_ada[l]).reshape(MOD_ROWS, 6, d)
        w_in_bf = w_in[l].astype(BF16)
        w_out_bf = w_out[l].astype(BF16)
        w_gu_bf = w_gu[l].astype(BF16)
        w_down_bf = w_down[l].astype(BF16)
        gq2 = jnp.tile(g_q[l], LANES // HEAD_DIM).reshape(1, LANES)
        gk2 = jnp.tile(g_k[l], LANES // HEAD_DIM).reshape(1, LANES)
        lam_rows = jnp.zeros((MOD_ROWS, LANES), F32).at[0:4, 0:HEAD_DIM].set(
            jnp.stack([lam_q1[l], lam_k1[l], lam_q2[l], lam_k2[l]]))
        new_xs = []
        row0 = 0
        for x, (cos2, sin2) in zip(xs, tables):
            mod3 = mod[row0:row0 + x.shape[0]]
            row0 += x.shape[0]
            act = _in_projection(x, mod3, g_pre_mix[l].reshape(1, d), w_in_bf, gq2, gk2,
                                 cos2, sin2, seg)
            out_a = _gqa_attention(act)
            out_d = _diff_attention(act, rel_bias, lam_rows, g_subln[l].reshape(1, LANES),
                                    lam_init)
            new_xs.append(_out_ffn(x, out_a, out_d, mod3, w_out_bf,
                                   g_post_mix[l].reshape(1, d), g_pre_ffn[l].reshape(1, d),
                                   w_gu_bf, w_down_bf, g_post_ffn[l].reshape(1, d)))
        xs = new_xs
    return tuple(xs)
```

```python
import functools
import math

import jax
import jax.numpy as jnp
from jax import lax
from jax.experimental import pallas as pl
from jax.experimental.pallas import tpu as pltpu

F32 = jnp.float32
BF16 = jnp.bfloat16

HEAD_DIM = 64
N_GQA_HEADS = 8
N_GQA_KV = 2
GQA_GROUP = N_GQA_HEADS // N_GQA_KV
N_DIFF_HEADS = 4
GRID_W = 64
NUM_BUCKETS = 32
MAX_DISTANCE = 128
ROPE_THETA = 10000.0
EPS = 1e-6
ATTN_SCALE = 1.0 / math.sqrt(HEAD_DIM)
LOG2E = 1.4426950408889634
Q_SCALE = ATTN_SCALE * LOG2E

LANES = 128
GQA_Q_COLS = N_GQA_HEADS * HEAD_DIM
GQA_KV_COLS = N_GQA_KV * HEAD_DIM
DIFF_COLS = N_DIFF_HEADS * 2 * HEAD_DIM

QA_BLK = 0
KA_BLK = QA_BLK + GQA_Q_COLS // LANES
VA_BLK = KA_BLK + N_GQA_KV
QD_BLK = VA_BLK + N_GQA_KV
KD_BLK = QD_BLK + N_DIFF_HEADS
VD_BLK = KD_BLK + N_DIFF_HEADS
ACT_COLS = (VD_BLK + N_DIFF_HEADS) * LANES

MOD_ROWS = 8
M_INIT = -0.5 * float(jnp.finfo(jnp.float32).max)

VMEM_LIMIT = 56 * 1024 * 1024
FLASH_UNROLL = 2
STRIP_ROWS = 32
BLOCK_ROWS = 512


def _rms(x, g):
    return x * lax.rsqrt(jnp.mean(x * x, axis=-1, keepdims=True) + EPS) * g


def _mod_kernel(c_ref, w_ref, b_ref, o_ref):
    c = c_ref[...]
    a = c * (1.0 / (1.0 + jnp.exp(-c)))
    a_hi = a.astype(BF16)
    a_lo = (a - a_hi.astype(F32)).astype(BF16)
    w = w_ref[...]
    w_hi = w.astype(BF16)
    w_lo = (w - w_hi.astype(F32)).astype(BF16)
    acc = jnp.dot(a_hi, w_hi, preferred_element_type=F32)
    acc += jnp.dot(a_lo, w_hi, preferred_element_type=F32)
    acc += jnp.dot(a_hi, w_lo, preferred_element_type=F32)
    o_ref[...] = acc + b_ref[...]


def _modulation(c_all, w_ada, b_ada):
    rows, d = c_all.shape
    n_out = w_ada.shape[1]
    tn = 1536
    return pl.pallas_call(
        _mod_kernel,
        grid=(n_out // tn,),
        in_specs=[
            pl.BlockSpec((rows, d), lambda j: (0, 0)),
            pl.BlockSpec((d, tn), lambda j: (0, j)),
            pl.BlockSpec((1, tn), lambda j: (0, j)),
        ],
        out_specs=pl.BlockSpec((rows, tn), lambda j: (0, j)),
        out_shape=jax.ShapeDtypeStruct((rows, n_out), F32),
        compiler_params=pltpu.CompilerParams(
            dimension_semantics=("arbitrary",), vmem_limit_bytes=VMEM_LIMIT),
        name="modulation",
    )(c_all, w_ada, b_ada.reshape(1, n_out))


def _inproj_kernel(x_ref, mod_ref, g_ref, w_ref, gq_ref, gk_ref, cos_ref, sin_ref, seg_ref,
                   o_ref):
    x = x_ref[0]
    sh = mod_ref[0, 0:1, :]
    sc = mod_ref[0, 1:2, :]
    h = _rms(x, g_ref[...]) * (1.0 + sc) + sh
    proj = jnp.dot(h.astype(BF16), w_ref[...], preferred_element_type=F32)

    tm = x.shape[0]
    cos = cos_ref[...]
    sin = sin_ref[...]
    seg = seg_ref[...]
    lane = lax.broadcasted_iota(jnp.int32, (tm, LANES), 1)
    first_half = (lane & 16) == 0
    low = lane < HEAD_DIM

    def norm_rope(t, g):
        ss = jnp.dot((t * t).astype(BF16), seg, preferred_element_type=F32)
        tn = t * lax.rsqrt(ss * (1.0 / HEAD_DIM) + EPS) * g
        rot = jnp.where(first_half, pltpu.roll(tn, LANES - 16, 1), pltpu.roll(tn, 16, 1))
        return tn * cos + rot * sin

    def put(blk, val):
        o_ref[0, :, blk * LANES:(blk + 1) * LANES] = val.astype(BF16)

    col = 0
    for j in range(GQA_Q_COLS // LANES):
        put(QA_BLK + j, norm_rope(proj[:, col:col + LANES], gq_ref[...]) * Q_SCALE)
        col += LANES
    kr = norm_rope(proj[:, col:col + LANES], gk_ref[...])
    col += LANES
    kr_sw = pltpu.roll(kr, HEAD_DIM, 1)
    put(KA_BLK + 0, jnp.where(low, kr, kr_sw))
    put(KA_BLK + 1, jnp.where(low, kr_sw, kr))
    va = proj[:, col:col + LANES]
    col += LANES
    va_sw = pltpu.roll(va, HEAD_DIM, 1)
    put(VA_BLK + 0, jnp.where(low, va, 1.0))
    put(VA_BLK + 1, jnp.where(low, va_sw, 1.0))
    for j in range(N_DIFF_HEADS):
        put(QD_BLK + j, proj[:, col:col + LANES] * Q_SCALE)
        col += LANES
    for j in range(N_DIFF_HEADS):
        put(KD_BLK + j, proj[:, col:col + LANES])
        col += LANES
    for j in range(N_DIFF_HEADS):
        put(VD_BLK + j, proj[:, col:col + LANES])
        col += LANES


def _in_projection(x, mod3, g_pre, w_in_bf, gq2, gk2, cos2, sin2, seg, tm=512):
    b, n, d = x.shape
    in_w = w_in_bf.shape[1]
    const = lambda bi, i: (0, 0)
    return pl.pallas_call(
        _inproj_kernel,
        grid=(b, n // tm),
        in_specs=[
            pl.BlockSpec((1, tm, d), lambda bi, i: (bi, i, 0)),
            pl.BlockSpec((1, 6, d), lambda bi, i: (bi, 0, 0)),
            pl.BlockSpec((1, d), const),
            pl.BlockSpec((d, in_w), const),
            pl.BlockSpec((1, LANES), const),
            pl.BlockSpec((1, LANES), const),
            pl.BlockSpec((tm, LANES), lambda bi, i: (i, 0)),
            pl.BlockSpec((tm, LANES), lambda bi, i: (i, 0)),
            pl.BlockSpec((LANES, LANES), const),
        ],
        out_specs=pl.BlockSpec((1, tm, ACT_COLS), lambda bi, i: (bi, i, 0)),
        out_shape=jax.ShapeDtypeStruct((b, n, ACT_COLS), BF16),
        compiler_params=pltpu.CompilerParams(
            dimension_semantics=("arbitrary", "arbitrary"), vmem_limit_bytes=VMEM_LIMIT),
        name="in_projection",
    )(x, mod3, g_pre, w_in_bf, gq2, gk2, cos2, sin2, seg)


def _flash_sweep(nc, tk, qz_ref, k_ref, v_ref, blocks, bufs, m_ref, acc_ref, *, v_ones=None,
                 n_tail=0, main_chunk=None, main_shift=None, tail_chunk=None, tail_bias=None):
    n_main = nc - n_tail
    unroll = min(FLASH_UNROLL, n_main) if n_main else 0
    assert n_tail % 2 == 0 and (n_main == 0 or (unroll % 2 == 0 and n_main % unroll == 0))

    def chunk_of(kind, idx):
        if kind == "tail":
            return tail_chunk(idx), None, tail_bias
        c = idx if main_chunk is None else main_chunk(idx)
        return c, (None if main_shift is None else main_shift(c)), None

    def scores(kind, idx, s_ref, mc_ref):
        c, shift, bias = chunk_of(kind, idx)
        start = pl.multiple_of(c * tk, tk)
        kc = k_ref[0, pl.ds(start, tk), :]
        for rows in blocks:
            s = lax.dot_general(qz_ref[rows, :], kc, (((1,), (1,)), ((), ())),
                                preferred_element_type=F32)
            if bias is not None:
                s = s + bias(c)
            s_ref[rows, :] = s
            mc = jnp.max(s, axis=-1, keepdims=True)
            if shift is not None:
                mc = mc + shift
            mc_ref[rows, :] = jnp.broadcast_to(mc, (s.shape[0], LANES))

    def values(kind, idx, s_ref, mc_ref):
        c, shift, _ = chunk_of(kind, idx)
        start = pl.multiple_of(c * tk, tk)
        vc = v_ref[0, pl.ds(start, tk), :]
        if v_ones is not None:
            vc = jnp.concatenate([vc, v_ones], axis=1)
        reps = acc_ref.shape[1] // LANES
        for rows in blocks:
            strips = []
            for r0 in range(rows.start, rows.stop, STRIP_ROWS):
                rs = slice(r0, r0 + STRIP_ROWS)
                m_prev = m_ref[rs, :]
                m_new = jnp.maximum(m_prev, mc_ref[rs, :])
                alpha = jnp.exp2(m_prev - m_new)
                m_ref[rs, :] = m_new
                for j in range(reps):
                    cols = slice(j * LANES, (j + 1) * LANES)
                    acc_ref[rs, cols] = alpha * acc_ref[rs, cols]
                m_sub = m_new if shift is None else m_new - shift
                strips.append(jnp.concatenate(
                    [jnp.exp2(s_ref[rs, j * LANES:(j + 1) * LANES] - m_sub).astype(BF16)
                     for j in range(tk // LANES)], axis=1))
            p = jnp.concatenate(strips, axis=0)
            acc_ref[rows, :] += jnp.dot(p, vc, preferred_element_type=F32)

    def group(kind, base, count, nxt):
        for u in range(count):
            if u + 1 < count:
                scores(kind, base + u + 1, *bufs[(u + 1) % 2])
            elif nxt is not None:
                scores(*nxt, *bufs[(u + 1) % 2])
            values(kind, base + u, *bufs[u % 2])

    tail_start = ("tail", 0) if n_tail else None
    scores(*(("main", 0) if n_main else tail_start), *bufs[0])
    if n_main:
        n_groups = n_main // unroll

        def body(g, carry):
            group("main", g * unroll, unroll, ("main", (g + 1) * unroll))
            return carry

        lax.fori_loop(0, n_groups - 1, body, 0)
        group("main", n_main - unroll, unroll, tail_start)
    if n_tail:
        group("tail", 0, n_tail, None)


def _flash_scratch(mrows, tk, acc_cols):
    return [
        pltpu.VMEM((mrows, LANES), BF16),
        pltpu.VMEM((mrows, LANES), F32),
        pltpu.VMEM((mrows, acc_cols), F32),
        pltpu.VMEM((mrows, tk), F32),
        pltpu.VMEM((mrows, tk), F32),
        pltpu.VMEM((mrows, LANES), F32),
        pltpu.VMEM((mrows, LANES), F32),
    ]


def _gqa_kernel(q_ref, k_ref, v_ref, o_ref, qz_ref, m_ref, acc_ref, s0_ref, s1_ref,
                mc0_ref, mc1_ref, *, tq, tk):
    n = k_ref.shape[1]
    mrows = GQA_GROUP * tq
    lane = lax.broadcasted_iota(jnp.int32, (tq, LANES), 1)
    low = lane < HEAD_DIM
    for g in range(GQA_GROUP):
        slab = q_ref[0, :, (g // 2) * LANES:(g // 2 + 1) * LANES]
        keep = low if g % 2 == 0 else jnp.logical_not(low)
        qz_ref[g * tq:(g + 1) * tq, :] = jnp.where(keep, slab, jnp.zeros_like(slab))
    m_ref[...] = jnp.full(m_ref.shape, M_INIT, F32)
    acc_ref[...] = jnp.zeros(acc_ref.shape, F32)

    blocks = tuple(slice(r, r + BLOCK_ROWS) for r in range(0, mrows, BLOCK_ROWS))
    _flash_sweep(n // tk, tk, qz_ref, k_ref, v_ref, blocks,
                 ((s0_ref, mc0_ref), (s1_ref, mc1_ref)), m_ref, acc_ref)

    outs = []
    for g in range(GQA_GROUP):
        a = acc_ref[g * tq:(g + 1) * tq, :]
        outs.append(a / pltpu.roll(a, HEAD_DIM, 1))
    for j in range(GQA_GROUP // 2):
        pair = jnp.where(low, outs[2 * j], pltpu.roll(outs[2 * j + 1], HEAD_DIM, 1))
        o_ref[0, :, j * LANES:(j + 1) * LANES] = pair.astype(o_ref.dtype)


def _gqa_attention(act, tq=512, tk=1024):
    b, n, _ = act.shape
    width = GQA_GROUP * HEAD_DIM
    mrows = GQA_GROUP * tq
    return pl.pallas_call(
        functools.partial(_gqa_kernel, tq=tq, tk=tk),
        grid=(b, N_GQA_KV, n // tq),
        in_specs=[
            pl.BlockSpec((1, tq, width), lambda bi, h, i: (bi, i, h)),
            pl.BlockSpec((1, n, LANES), lambda bi, h, i: (bi, 0, KA_BLK + h)),
            pl.BlockSpec((1, n, LANES), lambda bi, h, i: (bi, 0, VA_BLK + h)),
        ],
        out_specs=pl.BlockSpec((1, tq, width), lambda bi, h, i: (bi, i, h)),
        out_shape=jax.ShapeDtypeStruct((b, n, GQA_Q_COLS), BF16),
        scratch_shapes=_flash_scratch(mrows, tk, LANES),
        compiler_params=pltpu.CompilerParams(
            dimension_semantics=("arbitrary", "arbitrary", "arbitrary"),
            vmem_limit_bytes=VMEM_LIMIT),
        name="gqa_attention",
    )(act, act, act)


def _rel_bucket(rel):
    half = NUM_BUCKETS // 2
    max_exact = half // 2
    n = jnp.minimum(jnp.abs(rel), MAX_DISTANCE)
    n2 = n * n
    large = jnp.full(rel.shape, max_exact, jnp.int32)
    for k in range(1, half - max_exact):
        large = large + (n2 >= (max_exact * max_exact) * (2 ** k)).astype(jnp.int32)
    return jnp.where(rel > 0, half, 0) + jnp.where(n < max_exact, n, large)


def _bias_geometry(t, tk):
    assert tk % t == 0
    e_lo = (-(MAX_DISTANCE - 1) - tk) // t
    e_hi = -((-(MAX_DISTANCE - 1) - t) // t)
    per_tile = -(-(e_hi - e_lo - 1) // (tk // t))
    return e_lo, e_hi, per_tile + per_tile % 2


def _diff_kernel(rb_ref, lam_ref, gs_ref, q_ref, k_ref, v_ref, o_ref,
                 qz_ref, m_ref, acc_ref, s0_ref, s1_ref, mc0_ref, mc1_ref, bias_ref,
                 *, t, tk, lam_init):
    h = pl.program_id(0)
    bi = pl.program_id(1)
    i = pl.program_id(2)
    n = k_ref.shape[1]
    nc = n // tk
    ratio = tk // t
    e_lo, e_hi, near_chunks = _bias_geometry(t, tk)
    shift_before = rb_ref[NUM_BUCKETS // 2 - 1, h] * LOG2E
    shift_after = rb_ref[NUM_BUCKETS - 1, h] * LOG2E

    @pl.when(jnp.logical_and(bi == 0, i == 0))
    def _():
        width = (e_hi - 1 - e_lo) * t + tk
        bucket = _rel_bucket(e_lo * t + lax.broadcasted_iota(jnp.int32, (8, width), 1))
        gen = jnp.zeros((8, width), F32)
        for bkt in range(NUM_BUCKETS):
            gen = jnp.where(bucket == bkt, rb_ref[bkt, h] * LOG2E, gen)
        sub = lax.broadcasted_iota(jnp.int32, (8, width), 0)
        rows8 = gen
        for r in range(1, 8):
            rows8 = jnp.where(sub == r, pltpu.roll(gen, r, 1), rows8)
        for r0 in range(0, t, 8):
            blk = rows8 if r0 == 0 else pltpu.roll(rows8, r0, 1)
            for e in range(e_lo + 1, e_hi):
                off = (e - e_lo) * t
                bias_ref[e - e_lo, r0:r0 + 8, :] = blk[:, off:off + tk]
        bias_ref[0] = jnp.full((t, tk), shift_before, F32)
        bias_ref[e_hi - e_lo] = jnp.full((t, tk), shift_after, F32)

    lane = lax.broadcasted_iota(jnp.int32, (t, LANES), 1)
    low = lane < HEAD_DIM
    q = q_ref[0]
    zero = jnp.zeros_like(q)
    qz_ref[0:t, :] = jnp.where(low, q, zero)
    qz_ref[t:2 * t, :] = jnp.where(low, zero, q)
    m_ref[...] = jnp.full(m_ref.shape, M_INIT, F32)
    acc_ref[...] = jnp.zeros(acc_ref.shape, F32)

    n_near = min(near_chunks, nc)
    near0 = jnp.clip(jnp.floor_divide(i + e_lo, ratio) + 1, 0, nc - n_near)

    _flash_sweep(
        nc, tk, qz_ref, k_ref, v_ref, (slice(0, t), slice(t, 2 * t)),
        ((s0_ref, mc0_ref), (s1_ref, mc1_ref)), m_ref, acc_ref,
        v_ones=jnp.ones((tk, LANES), BF16), n_tail=n_near,
        main_chunk=lambda pos: pos + jnp.where(pos >= near0, n_near, 0),
        main_shift=lambda c: jnp.where(c < near0, shift_before, shift_after),
        tail_chunk=lambda u: near0 + u,
        tail_bias=lambda c: bias_ref[jnp.clip(ratio * c - i, e_lo, e_hi) - e_lo])

    lv = lam_ref[...]
    lam = (jnp.exp(jnp.sum(lv[0:1, :] * lv[1:2, :], axis=-1, keepdims=True))
           - jnp.exp(jnp.sum(lv[2:3, :] * lv[3:4, :], axis=-1, keepdims=True)) + lam_init)
    o1 = acc_ref[0:t, 0:LANES] / acc_ref[0:t, LANES:2 * LANES]
    o2 = acc_ref[t:2 * t, 0:LANES] / acc_ref[t:2 * t, LANES:2 * LANES]
    o = o1 - lam * o2
    o_ref[0] = (_rms(o, gs_ref[...]) * (1.0 - lam_init)).astype(o_ref.dtype)


def _diff_attention(act, rel_bias, lam_rows, g_subln, lam_init, t=512, tk=1024):
    b, n, _ = act.shape
    e_lo, e_hi, _ = _bias_geometry(t, tk)
    return pl.pallas_call(
        functools.partial(_diff_kernel, t=t, tk=tk, lam_init=lam_init),
        grid=(N_DIFF_HEADS, b, n // t),
        in_specs=[
            pl.BlockSpec(memory_space=pltpu.SMEM),
            pl.BlockSpec((MOD_ROWS, LANES), lambda h, bi, i: (0, 0)),
            pl.BlockSpec((1, LANES), lambda h, bi, i: (0, 0)),
            pl.BlockSpec((1, t, LANES), lambda h, bi, i: (bi, i, QD_BLK + h)),
            pl.BlockSpec((1, n, LANES), lambda h, bi, i: (bi, 0, KD_BLK + h)),
            pl.BlockSpec((1, n, LANES), lambda h, bi, i: (bi, 0, VD_BLK + h)),
        ],
        out_specs=pl.BlockSpec((1, t, LANES), lambda h, bi, i: (bi, i, h)),
        out_shape=jax.ShapeDtypeStruct((b, n, DIFF_COLS), BF16),
        scratch_shapes=_flash_scratch(2 * t, tk, 2 * LANES)
        + [pltpu.VMEM((e_hi - e_lo + 1, t, tk), F32)],
        compiler_params=pltpu.CompilerParams(
            dimension_semantics=("arbitrary", "arbitrary", "arbitrary"),
            vmem_limit_bytes=VMEM_LIMIT),
        name="diff_attention",
    )(rel_bias, lam_rows, g_subln, act, act, act)


def _ffn_kernel(x_ref, oa_ref, od_ref, mod_ref, wout_ref, gpm_ref, gpf_ref, wgu_ref, wdn_ref,
                gpo_ref, y_ref):
    x = x_ref[0]
    gt1 = mod_ref[0, 2:3, :]
    sh2 = mod_ref[0, 3:4, :]
    sc2 = mod_ref[0, 4:5, :]
    gt2 = mod_ref[0, 5:6, :]
    wa = oa_ref.shape[2]
    mix = jnp.dot(oa_ref[0], wout_ref[0:wa, :], preferred_element_type=F32)
    mix += jnp.dot(od_ref[0], wout_ref[wa:, :], preferred_element_type=F32)
    x1 = x + gt1 * _rms(mix, gpm_ref[...])
    h = _rms(x1, gpf_ref[...]) * (1.0 + sc2) + sh2
    gu = jnp.dot(h.astype(BF16), wgu_ref[...], preferred_element_type=F32)
    d_ff = wdn_ref.shape[0]
    gate = gu[:, :d_ff]
    up = gu[:, d_ff:]
    act = gate * (1.0 / (1.0 + jnp.exp(-gate))) * up
    f = jnp.dot(act.astype(BF16), wdn_ref[...], preferred_element_type=F32)
    y_ref[0] = x1 + gt2 * _rms(f, gpo_ref[...])


def _out_ffn(x, out_a, out_d, mod3, w_out_bf, g_post_mix, g_pre_ffn, w_gu_bf, w_down_bf,
             g_post_ffn, tm=256):
    b, n, d = x.shape
    const = lambda bi, i: (0, 0)
    once = pl.Buffered(1)
    return pl.pallas_call(
        _ffn_kernel,
        grid=(b, n // tm),
        in_specs=[
            pl.BlockSpec((1, tm, d), lambda bi, i: (bi, i, 0)),
            pl.BlockSpec((1, tm, out_a.shape[2]), lambda bi, i: (bi, i, 0)),
            pl.BlockSpec((1, tm, out_d.shape[2]), lambda bi, i: (bi, i, 0)),
            pl.BlockSpec((1, 6, d), lambda bi, i: (bi, 0, 0)),
            pl.BlockSpec(w_out_bf.shape, const, pipeline_mode=once),
            pl.BlockSpec((1, d), const),
            pl.BlockSpec((1, d), const),
            pl.BlockSpec(w_gu_bf.shape, const, pipeline_mode=once),
            pl.BlockSpec(w_down_bf.shape, const, pipeline_mode=once),
            pl.BlockSpec((1, d), const),
        ],
        out_specs=pl.BlockSpec((1, tm, d), lambda bi, i: (bi, i, 0)),
        out_shape=jax.ShapeDtypeStruct((b, n, d), F32),
        compiler_params=pltpu.CompilerParams(
            dimension_semantics=("arbitrary", "arbitrary"), vmem_limit_bytes=VMEM_LIMIT),
        name="out_ffn",
    )(x, out_a, out_d, mod3, w_out_bf, g_post_mix, g_pre_ffn, w_gu_bf, w_down_bf, g_post_ffn)


def _rope_tables(n):
    rows = n // GRID_W
    row = jnp.repeat(jnp.arange(rows), GRID_W).astype(F32)
    col = jnp.tile(jnp.arange(GRID_W), rows).astype(F32)
    half = HEAD_DIM // 2
    inv = ROPE_THETA ** (-jnp.arange(0, half, 2, dtype=F32) / half)
    ang_r = row[:, None] * inv[None, :]
    ang_c = col[:, None] * inv[None, :]
    ang = jnp.concatenate([ang_r, ang_r, ang_c, ang_c], axis=-1)
    sign = jnp.tile(jnp.concatenate([-jnp.ones((16,), F32), jnp.ones((16,), F32)]), 2)
    cos = jnp.cos(ang)
    sin = jnp.sin(ang) * sign[None, :]
    return jnp.tile(cos, (1, 2)), jnp.tile(sin, (1, 2))


def kernel(x_prompt, x_sample, c_prompt, c_sample, rel_bias, w_ada, b_ada, g_pre_mix, w_in,
           g_q, g_k, lam_q1, lam_k1, lam_q2, lam_k2, g_subln, w_out, g_post_mix, g_pre_ffn,
           w_gu, w_down, g_post_ffn):
    depth = w_ada.shape[0]
    d = x_prompt.shape[-1]
    xs = [x_prompt, x_sample]
    cs = [c_prompt, c_sample]
    n_c = sum(c.shape[0] for c in cs)
    assert n_c <= MOD_ROWS
    c_all = jnp.concatenate(cs + [jnp.zeros((MOD_ROWS - n_c, d), F32)], axis=0)
    seg = (jnp.arange(LANES)[:, None] // HEAD_DIM
           == jnp.arange(LANES)[None, :] // HEAD_DIM).astype(BF16)
    tables = [_rope_tables(x.shape[1]) for x in xs]

    for l in range(depth):
        lam_init = 0.8 - 0.6 * math.exp(-0.3 * l)
        mod = _modulation(c_all, w_ada[l], b_ada[l]).reshape(MOD_ROWS, 6, d)
        w_in_bf = w_in[l].astype(BF16)
        w_out_bf = w_out[l].astype(BF16)
        w_gu_bf = w_gu[l].astype(BF16)
        w_down_bf = w_down[l].astype(BF16)
        gq2 = jnp.tile(g_q[l], LANES // HEAD_DIM).reshape(1, LANES)
        gk2 = jnp.tile(g_k[l], LANES // HEAD_DIM).reshape(1, LANES)
        lam_rows = jnp.zeros((MOD_ROWS, LANES), F32).at[0:4, 0:HEAD_DIM].set(
            jnp.stack([lam_q1[l], lam_k1[l], lam_q2[l], lam_k2[l]]))
        new_xs = []
        row0 = 0
        for x, (cos2, sin2) in zip(xs, tables):
            mod3 = mod[row0:row0 + x.shape[0]]
            row0 += x.shape[0]
            act = _in_projection(x, mod3, g_pre_mix[l].reshape(1, d), w_in_bf, gq2, gk2,
                                 cos2, sin2, seg)
            out_a = _gqa_attention(act)
            out_d = _diff_attention(act, rel_bias, lam_rows, g_subln[l].reshape(1, LANES),
                                    lam_init)
            new_xs.append(_out_ffn(x, out_a, out_d, mod3, w_out_bf,
                                   g_post_mix[l].reshape(1, d), g_pre_ffn[l].reshape(1, d),
                                   w_gu_bf, w_down_bf, g_post_ffn[l].reshape(1, d)))
        xs = new_xs
    return tuple(xs)
```

```python
import functools
import math

import jax
import jax.numpy as jnp
from jax import lax
from jax.experimental import pallas as pl
from jax.experimental.pallas import tpu as pltpu

F32 = jnp.float32
BF16 = jnp.bfloat16

HEAD_DIM = 64
N_GQA_HEADS = 8
N_GQA_KV = 2
GQA_GROUP = N_GQA_HEADS // N_GQA_KV
N_DIFF_HEADS = 4
GRID_W = 64
NUM_BUCKETS = 32
MAX_DISTANCE = 128
ROPE_THETA = 10000.0
EPS = 1e-6
ATTN_SCALE = 1.0 / math.sqrt(HEAD_DIM)
LOG2E = 1.4426950408889634
Q_SCALE = ATTN_SCALE * LOG2E

LANES = 128
GQA_Q_COLS = N_GQA_HEADS * HEAD_DIM
GQA_KV_COLS = N_GQA_KV * HEAD_DIM
DIFF_COLS = N_DIFF_HEADS * 2 * HEAD_DIM

QA_BLK = 0
KA_BLK = QA_BLK + GQA_Q_COLS // LANES
VA_BLK = KA_BLK + N_GQA_KV
QD_BLK = VA_BLK + N_GQA_KV
KD_BLK = QD_BLK + N_DIFF_HEADS
VD_BLK = KD_BLK + N_DIFF_HEADS
ACT_COLS = (VD_BLK + N_DIFF_HEADS) * LANES

MOD_ROWS = 8
M_INIT = -0.5 * float(jnp.finfo(jnp.float32).max)

VMEM_LIMIT = 56 * 1024 * 1024
FLASH_UNROLL = 2
STRIP_ROWS = 32
BLOCK_ROWS = 512
FFN_GROUP_ROWS = 256


def _rms(x, g):
    return x * lax.rsqrt(jnp.mean(x * x, axis=-1, keepdims=True) + EPS) * g


def _mod_kernel(c_ref, w_ref, b_ref, o_ref):
    c = c_ref[...]
    a = c * (1.0 / (1.0 + jnp.exp(-c)))
    a_hi = a.astype(BF16)
    a_lo = (a - a_hi.astype(F32)).astype(BF16)
    w = w_ref[0]
    w_hi = w.astype(BF16)
    w_lo = (w - w_hi.astype(F32)).astype(BF16)
    acc = jnp.dot(a_hi, w_hi, preferred_element_type=F32)
    acc += jnp.dot(a_lo, w_hi, preferred_element_type=F32)
    acc += jnp.dot(a_hi, w_lo, preferred_element_type=F32)
    o_ref[...] = acc + b_ref[...]


def _modulation(c_all, w_ada, b_ada, layer):
    rows, d = c_all.shape
    n_out = w_ada.shape[2]
    tn = 1536
    return pl.pallas_call(
        _mod_kernel,
        grid=(n_out // tn,),
        in_specs=[
            pl.BlockSpec((rows, d), lambda j: (0, 0)),
            pl.BlockSpec((1, d, tn), lambda j: (layer, 0, j)),
            pl.BlockSpec((1, tn), lambda j: (0, j)),
        ],
        out_specs=pl.BlockSpec((rows, tn), lambda j: (0, j)),
        out_shape=jax.ShapeDtypeStruct((rows, n_out), F32),
        compiler_params=pltpu.CompilerParams(
            dimension_semantics=("arbitrary",), vmem_limit_bytes=VMEM_LIMIT),
        name="modulation",
    )(c_all, w_ada, b_ada.reshape(1, n_out))


def _inproj_kernel(x_ref, mod_ref, g_ref, w_ref, gq_ref, gk_ref, cos_ref, sin_ref, seg_ref,
                   o_ref):
    for r0 in range(0, x_ref.shape[1], FFN_GROUP_ROWS):
        _inproj_rows(slice(r0, r0 + FFN_GROUP_ROWS), x_ref, mod_ref, g_ref, w_ref, gq_ref,
                     gk_ref, cos_ref, sin_ref, seg_ref, o_ref)


def _inproj_rows(rows, x_ref, mod_ref, g_ref, w_ref, gq_ref, gk_ref, cos_ref, sin_ref, seg_ref,
                 o_ref):
    x = x_ref[0, rows, :]
    sh = mod_ref[0, 0:1, :]
    sc = mod_ref[0, 1:2, :]
    h = _rms(x, g_ref[...]) * (1.0 + sc) + sh
    proj = jnp.dot(h.astype(BF16), w_ref[...], preferred_element_type=F32)

    cos = cos_ref[rows, :]
    sin = sin_ref[rows, :]
    seg = seg_ref[...]
    lane = lax.broadcasted_iota(jnp.int32, (x.shape[0], LANES), 1)
    first_half = (lane & 16) == 0
    low = lane < HEAD_DIM

    def norm_rope(t, g):
        ss = jnp.dot((t * t).astype(BF16), seg, preferred_element_type=F32)
        tn = t * lax.rsqrt(ss * (1.0 / HEAD_DIM) + EPS) * g
        rot = jnp.where(first_half, pltpu.roll(tn, LANES - 16, 1), pltpu.roll(tn, 16, 1))
        return tn * cos + rot * sin

    def put(blk, val):
        o_ref[0, rows, blk * LANES:(blk + 1) * LANES] = val.astype(BF16)

    col = 0
    for j in range(GQA_Q_COLS // LANES):
        put(QA_BLK + j, norm_rope(proj[:, col:col + LANES], gq_ref[...]) * Q_SCALE)
        col += LANES
    kr = norm_rope(proj[:, col:col + LANES], gk_ref[...])
    col += LANES
    kr_sw = pltpu.roll(kr, HEAD_DIM, 1)
    put(KA_BLK + 0, jnp.where(low, kr, kr_sw))
    put(KA_BLK + 1, jnp.where(low, kr_sw, kr))
    va = proj[:, col:col + LANES]
    col += LANES
    va_sw = pltpu.roll(va, HEAD_DIM, 1)
    put(VA_BLK + 0, jnp.where(low, va, 1.0))
    put(VA_BLK + 1, jnp.where(low, va_sw, 1.0))
    for j in range(N_DIFF_HEADS):
        put(QD_BLK + j, proj[:, col:col + LANES] * Q_SCALE)
        col += LANES
    for j in range(N_DIFF_HEADS):
        put(KD_BLK + j, proj[:, col:col + LANES])
        col += LANES
    for j in range(N_DIFF_HEADS):
        put(VD_BLK + j, proj[:, col:col + LANES])
        col += LANES


def _in_projection(x, mod3, g_pre, w_in_bf, gq2, gk2, cos2, sin2, seg, tm=512):
    b, n, d = x.shape
    in_w = w_in_bf.shape[1]
    const = lambda bi, i: (0, 0)
    return pl.pallas_call(
        _inproj_kernel,
        grid=(b, n // tm),
        in_specs=[
            pl.BlockSpec((1, tm, d), lambda bi, i: (bi, i, 0)),
            pl.BlockSpec((1, 6, d), lambda bi, i: (bi, 0, 0)),
            pl.BlockSpec((1, d), const),
            pl.BlockSpec((d, in_w), const),
            pl.BlockSpec((1, LANES), const),
            pl.BlockSpec((1, LANES), const),
            pl.BlockSpec((tm, LANES), lambda bi, i: (i, 0)),
            pl.BlockSpec((tm, LANES), lambda bi, i: (i, 0)),
            pl.BlockSpec((LANES, LANES), const),
        ],
        out_specs=pl.BlockSpec((1, tm, ACT_COLS), lambda bi, i: (bi, i, 0)),
        out_shape=jax.ShapeDtypeStruct((b, n, ACT_COLS), BF16),
        compiler_params=pltpu.CompilerParams(
            dimension_semantics=("arbitrary", "arbitrary"), vmem_limit_bytes=VMEM_LIMIT),
        name="in_projection",
    )(x, mod3, g_pre, w_in_bf, gq2, gk2, cos2, sin2, seg)


def _flash_sweep(nc, tk, qz_ref, k_ref, v_ref, blocks, bufs, m_ref, acc_ref, *, v_ones=None,
                 n_tail=0, main_chunk=None, main_shift=None, tail_chunk=None, tail_bias=None):
    n_main = nc - n_tail
    unroll = min(FLASH_UNROLL, n_main) if n_main else 0
    assert n_tail % 2 == 0 and (n_main == 0 or (unroll % 2 == 0 and n_main % unroll == 0))

    def chunk_of(kind, idx):
        if kind == "tail":
            return tail_chunk(idx), None, tail_bias
        c = idx if main_chunk is None else main_chunk(idx)
        return c, (None if main_shift is None else main_shift(c)), None

    def scores(kind, idx, s_ref, mc_ref):
        c, shift, bias = chunk_of(kind, idx)
        start = pl.multiple_of(c * tk, tk)
        kc = k_ref[0, pl.ds(start, tk), :]
        for rows in blocks:
            s = lax.dot_general(qz_ref[rows, :], kc, (((1,), (1,)), ((), ())),
                                preferred_element_type=F32)
            if bias is not None:
                s = s + bias(c, rows)
            s_ref[rows, :] = s
            mc = jnp.max(s, axis=-1, keepdims=True)
            if shift is not None:
                mc = mc + shift
            mc_ref[rows, :] = jnp.broadcast_to(mc, (s.shape[0], LANES))

    def values(kind, idx, s_ref, mc_ref):
        c, shift, _ = chunk_of(kind, idx)
        start = pl.multiple_of(c * tk, tk)
        vc = v_ref[0, pl.ds(start, tk), :]
        if v_ones is not None:
            vc = jnp.concatenate([vc, v_ones], axis=1)
        reps = acc_ref.shape[1] // LANES
        for rows in blocks:
            strips = []
            for r0 in range(rows.start, rows.stop, STRIP_ROWS):
                rs = slice(r0, r0 + STRIP_ROWS)
                m_prev = m_ref[rs, :]
                m_new = jnp.maximum(m_prev, mc_ref[rs, :])
                alpha = jnp.exp2(m_prev - m_new)
                m_ref[rs, :] = m_new
                for j in range(reps):
                    cols = slice(j * LANES, (j + 1) * LANES)
                    acc_ref[rs, cols] = alpha * acc_ref[rs, cols]
                m_sub = m_new if shift is None else m_new - shift
                strips.append(jnp.concatenate(
                    [jnp.exp2(s_ref[rs, j * LANES:(j + 1) * LANES] - m_sub).astype(BF16)
                     for j in range(tk // LANES)], axis=1))
            p = jnp.concatenate(strips, axis=0)
            acc_ref[rows, :] += jnp.dot(p, vc, preferred_element_type=F32)

    def group(kind, base, count, nxt):
        for u in range(count):
            if u + 1 < count:
                scores(kind, base + u + 1, *bufs[(u + 1) % 2])
            elif nxt is not None:
                scores(*nxt, *bufs[(u + 1) % 2])
            values(kind, base + u, *bufs[u % 2])

    tail_start = ("tail", 0) if n_tail else None
    scores(*(("main", 0) if n_main else tail_start), *bufs[0])
    if n_main:
        n_groups = n_main // unroll

        def body(g, carry):
            group("main", g * unroll, unroll, ("main", (g + 1) * unroll))
            return carry

        lax.fori_loop(0, n_groups - 1, body, 0)
        group("main", n_main - unroll, unroll, tail_start)
    if n_tail:
        group("tail", 0, n_tail, None)


def _flash_scratch(mrows, tk, acc_cols):
    return [
        pltpu.VMEM((mrows, LANES), BF16),
        pltpu.VMEM((mrows, LANES), F32),
        pltpu.VMEM((mrows, acc_cols), F32),
        pltpu.VMEM((mrows, tk), F32),
        pltpu.VMEM((mrows, tk), F32),
        pltpu.VMEM((mrows, LANES), F32),
        pltpu.VMEM((mrows, LANES), F32),
    ]


def _gqa_kernel(q_ref, k_ref, v_ref, o_ref, qz_ref, m_ref, acc_ref, s0_ref, s1_ref,
                mc0_ref, mc1_ref, *, tq, tk):
    n = k_ref.shape[1]
    mrows = GQA_GROUP * tq
    lane = lax.broadcasted_iota(jnp.int32, (tq, LANES), 1)
    low = lane < HEAD_DIM
    for g in range(GQA_GROUP):
        slab = q_ref[0, :, (g // 2) * LANES:(g // 2 + 1) * LANES]
        keep = low if g % 2 == 0 else jnp.logical_not(low)
        qz_ref[g * tq:(g + 1) * tq, :] = jnp.where(keep, slab, jnp.zeros_like(slab))
    m_ref[...] = jnp.full(m_ref.shape, M_INIT, F32)
    acc_ref[...] = jnp.zeros(acc_ref.shape, F32)

    blocks = tuple(slice(r, r + BLOCK_ROWS) for r in range(0, mrows, BLOCK_ROWS))
    _flash_sweep(n // tk, tk, qz_ref, k_ref, v_ref, blocks,
                 ((s0_ref, mc0_ref), (s1_ref, mc1_ref)), m_ref, acc_ref)

    outs = []
    for g in range(GQA_GROUP):
        a = acc_ref[g * tq:(g + 1) * tq, :]
        outs.append(a / pltpu.roll(a, HEAD_DIM, 1))
    for j in range(GQA_GROUP // 2):
        pair = jnp.where(low, outs[2 * j], pltpu.roll(outs[2 * j + 1], HEAD_DIM, 1))
        o_ref[0, :, j * LANES:(j + 1) * LANES] = pair.astype(o_ref.dtype)


def _gqa_attention(act, tq=512, tk=1024):
    b, n, _ = act.shape
    width = GQA_GROUP * HEAD_DIM
    mrows = GQA_GROUP * tq
    return pl.pallas_call(
        functools.partial(_gqa_kernel, tq=tq, tk=tk),
        grid=(b, N_GQA_KV, n // tq),
        in_specs=[
            pl.BlockSpec((1, tq, width), lambda bi, h, i: (bi, i, h)),
            pl.BlockSpec((1, n, LANES), lambda bi, h, i: (bi, 0, KA_BLK + h)),
            pl.BlockSpec((1, n, LANES), lambda bi, h, i: (bi, 0, VA_BLK + h)),
        ],
        out_specs=pl.BlockSpec((1, tq, width), lambda bi, h, i: (bi, i, h)),
        out_shape=jax.ShapeDtypeStruct((b, n, GQA_Q_COLS), BF16),
        scratch_shapes=_flash_scratch(mrows, tk, LANES),
        compiler_params=pltpu.CompilerParams(
            dimension_semantics=("arbitrary", "arbitrary", "arbitrary"),
            vmem_limit_bytes=VMEM_LIMIT),
        name="gqa_attention",
    )(act, act, act)


def _rel_bucket(rel):
    half = NUM_BUCKETS // 2
    max_exact = half // 2
    n = jnp.minimum(jnp.abs(rel), MAX_DISTANCE)
    n2 = n * n
    large = jnp.full(rel.shape, max_exact, jnp.int32)
    for k in range(1, half - max_exact):
        large = large + (n2 >= (max_exact * max_exact) * (2 ** k)).astype(jnp.int32)
    return jnp.where(rel > 0, half, 0) + jnp.where(n < max_exact, n, large)


def _bias_geometry(tb, tk, n_sub):
    assert tk % tb == 0
    ratio = tk // tb
    e_lo = (-(MAX_DISTANCE - 1) - tk) // tb
    e_hi = -((-(MAX_DISTANCE - 1) - tb) // tb)
    spans = [-(-(j + n_sub - 1 + e_hi) // ratio) - 1 - (j + e_lo) // ratio
             for j in range(0, ratio * n_sub, n_sub)]
    return e_lo, e_hi, max(spans) + max(spans) % 2


def _diff_kernel(rb_ref, lam_ref, gs_ref, q_ref, k_ref, v_ref, o_ref,
                 qz_ref, m_ref, acc_ref, s0_ref, s1_ref, mc0_ref, mc1_ref, bias_ref,
                 *, t, tk, lam_init):
    h = pl.program_id(0)
    bi = pl.program_id(1)
    i = pl.program_id(2)
    n = k_ref.shape[1]
    nc = n // tk
    tb = BLOCK_ROWS
    n_sub = t // tb
    ratio = tk // tb
    e_lo, e_hi, near_chunks = _bias_geometry(tb, tk, n_sub)
    shift_before = rb_ref[NUM_BUCKETS // 2 - 1, h] * LOG2E
    shift_after = rb_ref[NUM_BUCKETS - 1, h] * LOG2E

    @pl.when(jnp.logical_and(bi == 0, i == 0))
    def _():
        width = (e_hi - 1 - e_lo) * tb + tk
        bucket = _rel_bucket(e_lo * tb + lax.broadcasted_iota(jnp.int32, (8, width), 1))
        gen = jnp.zeros((8, width), F32)
        for bkt in range(NUM_BUCKETS):
            gen = jnp.where(bucket == bkt, rb_ref[bkt, h] * LOG2E, gen)
        sub = lax.broadcasted_iota(jnp.int32, (8, width), 0)
        rows8 = gen
        for r in range(1, 8):
            rows8 = jnp.where(sub == r, pltpu.roll(gen, r, 1), rows8)
        for r0 in range(0, tb, 8):
            blk = rows8 if r0 == 0 else pltpu.roll(rows8, r0, 1)
            for e in range(e_lo + 1, e_hi):
                off = (e - e_lo) * tb
                bias_ref[e - e_lo, r0:r0 + 8, :] = blk[:, off:off + tk]
        bias_ref[0] = jnp.full((tb, tk), shift_before, F32)
        bias_ref[e_hi - e_lo] = jnp.full((tb, tk), shift_after, F32)

    lane = lax.broadcasted_iota(jnp.int32, (t, LANES), 1)
    low = lane < HEAD_DIM
    q = q_ref[0]
    zero = jnp.zeros_like(q)
    qz_ref[0:t, :] = jnp.where(low, q, zero)
    qz_ref[t:2 * t, :] = jnp.where(low, zero, q)
    m_ref[...] = jnp.full(m_ref.shape, M_INIT, F32)
    acc_ref[...] = jnp.zeros(acc_ref.shape, F32)

    n_near = min(near_chunks, nc)
    near0 = jnp.clip(jnp.floor_divide(i * n_sub + e_lo, ratio) + 1, 0, nc - n_near)

    def tail_bias(c, rows):
        block = i * n_sub + (rows.start % t) // tb
        return bias_ref[jnp.clip(ratio * c - block, e_lo, e_hi) - e_lo]

    _flash_sweep(
        nc, tk, qz_ref, k_ref, v_ref,
        tuple(slice(r, r + BLOCK_ROWS) for r in range(0, 2 * t, BLOCK_ROWS)),
        ((s0_ref, mc0_ref), (s1_ref, mc1_ref)), m_ref, acc_ref,
        v_ones=jnp.ones((tk, LANES), BF16), n_tail=n_near,
        main_chunk=lambda pos: pos + jnp.where(pos >= near0, n_near, 0),
        main_shift=lambda c: jnp.where(c < near0, shift_before, shift_after),
        tail_chunk=lambda u: near0 + u,
        tail_bias=tail_bias)

    lv = lam_ref[...]
    lam = (jnp.exp(jnp.sum(lv[0:1, :] * lv[1:2, :], axis=-1, keepdims=True))
           - jnp.exp(jnp.sum(lv[2:3, :] * lv[3:4, :], axis=-1, keepdims=True)) + lam_init)
    o1 = acc_ref[0:t, 0:LANES] / acc_ref[0:t, LANES:2 * LANES]
    o2 = acc_ref[t:2 * t, 0:LANES] / acc_ref[t:2 * t, LANES:2 * LANES]
    o = o1 - lam * o2
    o_ref[0] = (_rms(o, gs_ref[...]) * (1.0 - lam_init)).astype(o_ref.dtype)


def _diff_attention(act, rel_bias, lam_rows, g_subln, lam_init, t=1024, tk=1024):
    b, n, _ = act.shape
    e_lo, e_hi, _ = _bias_geometry(BLOCK_ROWS, tk, t // BLOCK_ROWS)
    return pl.pallas_call(
        functools.partial(_diff_kernel, t=t, tk=tk, lam_init=lam_init),
        grid=(N_DIFF_HEADS, b, n // t),
        in_specs=[
            pl.BlockSpec(memory_space=pltpu.SMEM),
            pl.BlockSpec((MOD_ROWS, LANES), lambda h, bi, i: (0, 0)),
            pl.BlockSpec((1, LANES), lambda h, bi, i: (0, 0)),
            pl.BlockSpec((1, t, LANES), lambda h, bi, i: (bi, i, QD_BLK + h)),
            pl.BlockSpec((1, n, LANES), lambda h, bi, i: (bi, 0, KD_BLK + h),
                         pipeline_mode=pl.Buffered(1)),
            pl.BlockSpec((1, n, LANES), lambda h, bi, i: (bi, 0, VD_BLK + h),
                         pipeline_mode=pl.Buffered(1)),
        ],
        out_specs=pl.BlockSpec((1, t, LANES), lambda h, bi, i: (bi, i, h)),
        out_shape=jax.ShapeDtypeStruct((b, n, DIFF_COLS), BF16),
        scratch_shapes=_flash_scratch(2 * t, tk, 2 * LANES)
        + [pltpu.VMEM((e_hi - e_lo + 1, BLOCK_ROWS, tk), F32)],
        compiler_params=pltpu.CompilerParams(
            dimension_semantics=("arbitrary", "arbitrary", "arbitrary"),
            vmem_limit_bytes=VMEM_LIMIT),
        name="diff_attention",
    )(rel_bias, lam_rows, g_subln, act, act, act)


def _ffn_kernel(x_ref, oa_ref, od_ref, mod_ref, wout_ref, gpm_ref, gpf_ref, wgu_ref, wdn_ref,
                gpo_ref, y_ref):
    gt1 = mod_ref[0, 2:3, :]
    sh2 = mod_ref[0, 3:4, :]
    sc2 = mod_ref[0, 4:5, :]
    gt2 = mod_ref[0, 5:6, :]
    wa = oa_ref.shape[2]
    d_ff = wdn_ref.shape[0]
    tm = x_ref.shape[1]
    for r0 in range(0, tm, FFN_GROUP_ROWS):
        rows = slice(r0, r0 + FFN_GROUP_ROWS)
        mix = jnp.dot(oa_ref[0, rows, :], wout_ref[0:wa, :], preferred_element_type=F32)
        mix += jnp.dot(od_ref[0, rows, :], wout_ref[wa:, :], preferred_element_type=F32)
        x1 = x_ref[0, rows, :] + gt1 * _rms(mix, gpm_ref[...])
        h = _rms(x1, gpf_ref[...]) * (1.0 + sc2) + sh2
        gu = jnp.dot(h.astype(BF16), wgu_ref[...], preferred_element_type=F32)
        gate = gu[:, :d_ff]
        up = gu[:, d_ff:]
        act = gate * (1.0 / (1.0 + jnp.exp(-gate))) * up
        f = jnp.dot(act.astype(BF16), wdn_ref[...], preferred_element_type=F32)
        y_ref[0, rows, :] = x1 + gt2 * _rms(f, gpo_ref[...])


def _out_ffn(x, out_a, out_d, mod3, w_out_bf, g_post_mix, g_pre_ffn, w_gu_bf, w_down_bf,
             g_post_ffn, tm=512):
    b, n, d = x.shape
    const = lambda bi, i: (0, 0)
    once = pl.Buffered(1)
    return pl.pallas_call(
        _ffn_kernel,
        grid=(b, n // tm),
        in_specs=[
            pl.BlockSpec((1, tm, d), lambda bi, i: (bi, i, 0)),
            pl.BlockSpec((1, tm, out_a.shape[2]), lambda bi, i: (bi, i, 0)),
            pl.BlockSpec((1, tm, out_d.shape[2]), lambda bi, i: (bi, i, 0)),
            pl.BlockSpec((1, 6, d), lambda bi, i: (bi, 0, 0)),
            pl.BlockSpec(w_out_bf.shape, const, pipeline_mode=once),
            pl.BlockSpec((1, d), const),
            pl.BlockSpec((1, d), const),
            pl.BlockSpec(w_gu_bf.shape, const, pipeline_mode=once),
            pl.BlockSpec(w_down_bf.shape, const, pipeline_mode=once),
            pl.BlockSpec((1, d), const),
        ],
        out_specs=pl.BlockSpec((1, tm, d), lambda bi, i: (bi, i, 0)),
        out_shape=jax.ShapeDtypeStruct((b, n, d), F32),
        compiler_params=pltpu.CompilerParams(
            dimension_semantics=("arbitrary", "arbitrary"), vmem_limit_bytes=VMEM_LIMIT),
        name="out_ffn",
    )(x, out_a, out_d, mod3, w_out_bf, g_post_mix, g_pre_ffn, w_gu_bf, w_down_bf, g_post_ffn)


def _rope_tables(n):
    rows = n // GRID_W
    row = jnp.repeat(jnp.arange(rows), GRID_W).astype(F32)
    col = jnp.tile(jnp.arange(GRID_W), rows).astype(F32)
    half = HEAD_DIM // 2
    inv = ROPE_THETA ** (-jnp.arange(0, half, 2, dtype=F32) / half)
    ang_r = row[:, None] * inv[None, :]
    ang_c = col[:, None] * inv[None, :]
    ang = jnp.concatenate([ang_r, ang_r, ang_c, ang_c], axis=-1)
    sign = jnp.tile(jnp.concatenate([-jnp.ones((16,), F32), jnp.ones((16,), F32)]), 2)
    cos = jnp.cos(ang)
    sin = jnp.sin(ang) * sign[None, :]
    return jnp.tile(cos, (1, 2)), jnp.tile(sin, (1, 2))


def kernel(x_prompt, x_sample, c_prompt, c_sample, rel_bias, w_ada, b_ada, g_pre_mix, w_in,
           g_q, g_k, lam_q1, lam_k1, lam_q2, lam_k2, g_subln, w_out, g_post_mix, g_pre_ffn,
           w_gu, w_down, g_post_ffn):
    depth = w_ada.shape[0]
    d = x_prompt.shape[-1]
    xs = [x_prompt, x_sample]
    cs = [c_prompt, c_sample]
    n_c = sum(c.shape[0] for c in cs)
    assert n_c <= MOD_ROWS
    c_all = jnp.concatenate(cs + [jnp.zeros((MOD_ROWS - n_c, d), F32)], axis=0)
    seg = (jnp.arange(LANES)[:, None] // HEAD_DIM
           == jnp.arange(LANES)[None, :] // HEAD_DIM).astype(BF16)
    tables = [_rope_tables(max(x.shape[1] for x in xs))] * len(xs)

    for l in range(depth):
        lam_init = 0.8 - 0.6 * math.exp(-0.3 * l)
        mod = _modulation(c_all, w_ada, b_ada[l], l).reshape(MOD_ROWS, 6, d)
        w_in_bf = w_in[l].astype(BF16)
        w_out_bf = w_out[l].astype(BF16)
        w_gu_bf = w_gu[l].astype(BF16)
        w_down_bf = w_down[l].astype(BF16)
        gq2 = jnp.tile(g_q[l], LANES // HEAD_DIM).reshape(1, LANES)
        gk2 = jnp.tile(g_k[l], LANES // HEAD_DIM).reshape(1, LANES)
        lam_rows = jnp.zeros((MOD_ROWS, LANES), F32).at[0:4, 0:HEAD_DIM].set(
            jnp.stack([lam_q1[l], lam_k1[l], lam_q2[l], lam_k2[l]]))
        new_xs = []
        row0 = 0
        for x, (cos2, sin2) in zip(xs, tables):
            mod3 = mod[row0:row0 + x.shape[0]]
            row0 += x.shape[0]
            act = _in_projection(x, mod3, g_pre_mix[l].reshape(1, d), w_in_bf, gq2, gk2,
                                 cos2, sin2, seg)
            out_a = _gqa_attention(act)
            out_d = _diff_attention(act, rel_bias, lam_rows, g_subln[l].reshape(1, LANES),
                                    lam_init)
            new_xs.append(_out_ffn(x, out_a, out_d, mod3, w_out_bf,
                                   g_post_mix[l].reshape(1, d), g_pre_ffn[l].reshape(1, d),
                                   w_gu_bf, w_down_bf, g_post_ffn[l].reshape(1, d)))
        xs = new_xs
    return tuple(xs)
```

```python
import functools
import math

import jax
import jax.numpy as jnp
from jax import lax
from jax.experimental import pallas as pl
from jax.experimental.pallas import tpu as pltpu

F32 = jnp.float32
BF16 = jnp.bfloat16

HEAD_DIM = 64
N_GQA_HEADS = 8
N_GQA_KV = 2
GQA_GROUP = N_GQA_HEADS // N_GQA_KV
N_DIFF_HEADS = 4
GRID_W = 64
NUM_BUCKETS = 32
MAX_DISTANCE = 128
ROPE_THETA = 10000.0
EPS = 1e-6
ATTN_SCALE = 1.0 / math.sqrt(HEAD_DIM)
LOG2E = 1.4426950408889634
Q_SCALE = ATTN_SCALE * LOG2E

LANES = 128
GQA_Q_COLS = N_GQA_HEADS * HEAD_DIM
GQA_KV_COLS = N_GQA_KV * HEAD_DIM
DIFF_COLS = N_DIFF_HEADS * 2 * HEAD_DIM

QA_BLK = 0
KA_BLK = QA_BLK + GQA_Q_COLS // LANES
VA_BLK = KA_BLK + N_GQA_KV
QD_BLK = VA_BLK + N_GQA_KV
KD_BLK = QD_BLK + N_DIFF_HEADS
VD_BLK = KD_BLK + N_DIFF_HEADS
ACT_COLS = (VD_BLK + N_DIFF_HEADS) * LANES

MOD_ROWS = 8
M_INIT = -0.5 * float(jnp.finfo(jnp.float32).max)

VMEM_LIMIT = 56 * 1024 * 1024
FLASH_UNROLL = 2
STRIP_ROWS = 32
BLOCK_ROWS = 512
FFN_GROUP_ROWS = 256


def _rms(x, g):
    return x * lax.rsqrt(jnp.mean(x * x, axis=-1, keepdims=True) + EPS) * g


def _mod_kernel(c_ref, w_ref, b_ref, o_ref):
    c = c_ref[...]
    a = c * (1.0 / (1.0 + jnp.exp(-c)))
    a_hi = a.astype(BF16)
    a_lo = (a - a_hi.astype(F32)).astype(BF16)
    w = w_ref[0]
    w_hi = w.astype(BF16)
    w_lo = (w - w_hi.astype(F32)).astype(BF16)
    acc = jnp.dot(a_hi, w_hi, preferred_element_type=F32)
    acc += jnp.dot(a_lo, w_hi, preferred_element_type=F32)
    acc += jnp.dot(a_hi, w_lo, preferred_element_type=F32)
    o_ref[...] = acc + b_ref[...]


def _modulation(c_all, w_ada, b_ada, layer):
    rows, d = c_all.shape
    n_out = w_ada.shape[2]
    tn = 1536
    return pl.pallas_call(
        _mod_kernel,
        grid=(n_out // tn,),
        in_specs=[
            pl.BlockSpec((rows, d), lambda j: (0, 0)),
            pl.BlockSpec((1, d, tn), lambda j: (layer, 0, j)),
            pl.BlockSpec((1, tn), lambda j: (0, j)),
        ],
        out_specs=pl.BlockSpec((rows, tn), lambda j: (0, j)),
        out_shape=jax.ShapeDtypeStruct((rows, n_out), F32),
        compiler_params=pltpu.CompilerParams(
            dimension_semantics=("arbitrary",), vmem_limit_bytes=VMEM_LIMIT),
        name="modulation",
    )(c_all, w_ada, b_ada.reshape(1, n_out))


def _inproj_kernel(x_ref, mod_ref, g_ref, w_ref, gq_ref, gk_ref, cos_ref, sin_ref, seg_ref,
                   o_ref):
    for r0 in range(0, x_ref.shape[1], FFN_GROUP_ROWS):
        _inproj_rows(slice(r0, r0 + FFN_GROUP_ROWS), x_ref, mod_ref, g_ref, w_ref, gq_ref,
                     gk_ref, cos_ref, sin_ref, seg_ref, o_ref)


def _inproj_rows(rows, x_ref, mod_ref, g_ref, w_ref, gq_ref, gk_ref, cos_ref, sin_ref, seg_ref,
                 o_ref):
    x = x_ref[0, rows, :]
    sh = mod_ref[0, 0:1, :]
    sc = mod_ref[0, 1:2, :]
    h = _rms(x, g_ref[...]) * (1.0 + sc) + sh
    proj = jnp.dot(h.astype(BF16), w_ref[...], preferred_element_type=F32)

    cos = cos_ref[rows, :]
    sin = sin_ref[rows, :]
    seg = seg_ref[...]
    lane = lax.broadcasted_iota(jnp.int32, (x.shape[0], LANES), 1)
    first_half = (lane & 16) == 0
    low = lane < HEAD_DIM

    def norm_rope(t, g):
        ss = jnp.dot((t * t).astype(BF16), seg, preferred_element_type=F32)
        tn = t * lax.rsqrt(ss * (1.0 / HEAD_DIM) + EPS) * g
        rot = jnp.where(first_half, pltpu.roll(tn, LANES - 16, 1), pltpu.roll(tn, 16, 1))
        return tn * cos + rot * sin

    def put(blk, val):
        o_ref[0, rows, blk * LANES:(blk + 1) * LANES] = val.astype(BF16)

    col = 0
    for j in range(GQA_Q_COLS // LANES):
        put(QA_BLK + j, norm_rope(proj[:, col:col + LANES], gq_ref[...]) * Q_SCALE)
        col += LANES
    kr = norm_rope(proj[:, col:col + LANES], gk_ref[...])
    col += LANES
    kr_sw = pltpu.roll(kr, HEAD_DIM, 1)
    put(KA_BLK + 0, jnp.where(low, kr, kr_sw))
    put(KA_BLK + 1, jnp.where(low, kr_sw, kr))
    va = proj[:, col:col + LANES]
    col += LANES
    va_sw = pltpu.roll(va, HEAD_DIM, 1)
    put(VA_BLK + 0, jnp.where(low, va, 1.0))
    put(VA_BLK + 1, jnp.where(low, va_sw, 1.0))
    for j in range(N_DIFF_HEADS):
        put(QD_BLK + j, proj[:, col:col + LANES] * Q_SCALE)
        col += LANES
    for j in range(N_DIFF_HEADS):
        put(KD_BLK + j, proj[:, col:col + LANES])
        col += LANES
    for j in range(N_DIFF_HEADS):
        put(VD_BLK + j, proj[:, col:col + LANES])
        col += LANES


def _in_projection(x, mod3, g_pre, w_in_bf, gq2, gk2, cos2, sin2, seg, tm=512):
    b, n, d = x.shape
    in_w = w_in_bf.shape[1]
    const = lambda bi, i: (0, 0)
    return pl.pallas_call(
        _inproj_kernel,
        grid=(b, n // tm),
        in_specs=[
            pl.BlockSpec((1, tm, d), lambda bi, i: (bi, i, 0)),
            pl.BlockSpec((1, 6, d), lambda bi, i: (bi, 0, 0)),
            pl.BlockSpec((1, d), const),
            pl.BlockSpec((d, in_w), const),
            pl.BlockSpec((1, LANES), const),
            pl.BlockSpec((1, LANES), const),
            pl.BlockSpec((tm, LANES), lambda bi, i: (i, 0)),
            pl.BlockSpec((tm, LANES), lambda bi, i: (i, 0)),
            pl.BlockSpec((LANES, LANES), const),
        ],
        out_specs=pl.BlockSpec((1, tm, ACT_COLS), lambda bi, i: (bi, i, 0)),
        out_shape=jax.ShapeDtypeStruct((b, n, ACT_COLS), BF16),
        compiler_params=pltpu.CompilerParams(
            dimension_semantics=("arbitrary", "arbitrary"), vmem_limit_bytes=VMEM_LIMIT),
        name="in_projection",
    )(x, mod3, g_pre, w_in_bf, gq2, gk2, cos2, sin2, seg)


def _flash_sweep(nc, tk, qz_ref, k_ref, v_ref, blocks, bufs, m_ref, acc_ref, *, v_ones=None,
                 n_tail=0, main_chunk=None, main_shift=None, tail_chunk=None, tail_bias=None):
    n_main = nc - n_tail
    unroll = min(FLASH_UNROLL, n_main) if n_main else 0
    assert n_tail % 2 == 0 and (n_main == 0 or (unroll % 2 == 0 and n_main % unroll == 0))

    def chunk_of(kind, idx):
        if kind == "tail":
            return tail_chunk(idx), None, tail_bias
        c = idx if main_chunk is None else main_chunk(idx)
        return c, (None if main_shift is None else main_shift(c)), None

    def scores(kind, idx, s_ref, mc_ref):
        c, shift, bias = chunk_of(kind, idx)
        start = pl.multiple_of(c * tk, tk)
        kc = k_ref[0, pl.ds(start, tk), :]
        for rows in blocks:
            s = lax.dot_general(qz_ref[rows, :], kc, (((1,), (1,)), ((), ())),
                                preferred_element_type=F32)
            if bias is not None:
                s = s + bias(c, rows)
            s_ref[rows, :] = s
            mc = jnp.max(s, axis=-1, keepdims=True)
            if shift is not None:
                mc = mc + shift
            mc_ref[rows, :] = jnp.broadcast_to(mc, (s.shape[0], LANES))

    def values(kind, idx, s_ref, mc_ref):
        c, shift, _ = chunk_of(kind, idx)
        start = pl.multiple_of(c * tk, tk)
        vc = v_ref[0, pl.ds(start, tk), :]
        if v_ones is not None:
            vc = jnp.concatenate([vc, v_ones], axis=1)
        reps = acc_ref.shape[1] // LANES
        for rows in blocks:
            strips = []
            for r0 in range(rows.start, rows.stop, STRIP_ROWS):
                rs = slice(r0, r0 + STRIP_ROWS)
                m_prev = m_ref[rs, :]
                m_new = jnp.maximum(m_prev, mc_ref[rs, :])
                alpha = jnp.exp2(m_prev - m_new)
                m_ref[rs, :] = m_new
                for j in range(reps):
                    cols = slice(j * LANES, (j + 1) * LANES)
                    acc_ref[rs, cols] = alpha * acc_ref[rs, cols]
                m_sub = m_new if shift is None else m_new - shift
                strips.append(jnp.concatenate(
                    [jnp.exp2(s_ref[rs, j * LANES:(j + 1) * LANES] - m_sub).astype(BF16)
                     for j in range(tk // LANES)], axis=1))
            p = jnp.concatenate(strips, axis=0)
            acc_ref[rows, :] += jnp.dot(p, vc, preferred_element_type=F32)

    def group(kind, base, count, nxt):
        for u in range(count):
            if u + 1 < count:
                scores(kind, base + u + 1, *bufs[(u + 1) % 2])
            elif nxt is not None:
                scores(*nxt, *bufs[(u + 1) % 2])
            values(kind, base + u, *bufs[u % 2])

    tail_start = ("tail", 0) if n_tail else None
    scores(*(("main", 0) if n_main else tail_start), *bufs[0])
    if n_main:
        n_groups = n_main // unroll

        def body(g, carry):
            group("main", g * unroll, unroll, ("main", (g + 1) * unroll))
            return carry

        lax.fori_loop(0, n_groups - 1, body, 0)
        group("main", n_main - unroll, unroll, tail_start)
    if n_tail:
        group("tail", 0, n_tail, None)


def _flash_scratch(mrows, tk, acc_cols):
    return [
        pltpu.VMEM((mrows, LANES), BF16),
        pltpu.VMEM((mrows, LANES), F32),
        pltpu.VMEM((mrows, acc_cols), F32),
        pltpu.VMEM((mrows, tk), F32),
        pltpu.VMEM((mrows, tk), F32),
        pltpu.VMEM((mrows, LANES), F32),
        pltpu.VMEM((mrows, LANES), F32),
    ]


def _gqa_kernel(q_ref, k_ref, v_ref, o_ref, qz_ref, m_ref, acc_ref, s0_ref, s1_ref,
                mc0_ref, mc1_ref, *, tq, tk):
    n = k_ref.shape[1]
    mrows = GQA_GROUP * tq
    lane = lax.broadcasted_iota(jnp.int32, (tq, LANES), 1)
    low = lane < HEAD_DIM
    for g in range(GQA_GROUP):
        slab = q_ref[0, :, (g // 2) * LANES:(g // 2 + 1) * LANES]
        keep = low if g % 2 == 0 else jnp.logical_not(low)
        qz_ref[g * tq:(g + 1) * tq, :] = jnp.where(keep, slab, jnp.zeros_like(slab))
    m_ref[...] = jnp.full(m_ref.shape, M_INIT, F32)
    acc_ref[...] = jnp.zeros(acc_ref.shape, F32)

    blocks = tuple(slice(r, r + BLOCK_ROWS) for r in range(0, mrows, BLOCK_ROWS))
    _flash_sweep(n // tk, tk, qz_ref, k_ref, v_ref, blocks,
                 ((s0_ref, mc0_ref), (s1_ref, mc1_ref)), m_ref, acc_ref)

    outs = []
    for g in range(GQA_GROUP):
        a = acc_ref[g * tq:(g + 1) * tq, :]
        outs.append(a / pltpu.roll(a, HEAD_DIM, 1))
    for j in range(GQA_GROUP // 2):
        pair = jnp.where(low, outs[2 * j], pltpu.roll(outs[2 * j + 1], HEAD_DIM, 1))
        o_ref[0, :, j * LANES:(j + 1) * LANES] = pair.astype(o_ref.dtype)


def _gqa_attention(act, tq=512, tk=1024):
    b, n, _ = act.shape
    width = GQA_GROUP * HEAD_DIM
    mrows = GQA_GROUP * tq
    return pl.pallas_call(
        functools.partial(_gqa_kernel, tq=tq, tk=tk),
        grid=(b, N_GQA_KV, n // tq),
        in_specs=[
            pl.BlockSpec((1, tq, width), lambda bi, h, i: (bi, i, h)),
            pl.BlockSpec((1, n, LANES), lambda bi, h, i: (bi, 0, KA_BLK + h)),
            pl.BlockSpec((1, n, LANES), lambda bi, h, i: (bi, 0, VA_BLK + h)),
        ],
        out_specs=pl.BlockSpec((1, tq, width), lambda bi, h, i: (bi, i, h)),
        out_shape=jax.ShapeDtypeStruct((b, n, GQA_Q_COLS), BF16),
        scratch_shapes=_flash_scratch(mrows, tk, LANES),
        compiler_params=pltpu.CompilerParams(
            dimension_semantics=("arbitrary", "arbitrary", "arbitrary"),
            vmem_limit_bytes=VMEM_LIMIT),
        name="gqa_attention",
    )(act, act, act)


def _rel_bucket(rel):
    half = NUM_BUCKETS // 2
    max_exact = half // 2
    n = jnp.minimum(jnp.abs(rel), MAX_DISTANCE)
    n2 = n * n
    large = jnp.full(rel.shape, max_exact, jnp.int32)
    for k in range(1, half - max_exact):
        large = large + (n2 >= (max_exact * max_exact) * (2 ** k)).astype(jnp.int32)
    return jnp.where(rel > 0, half, 0) + jnp.where(n < max_exact, n, large)


def _bias_geometry(tb, tk, n_sub):
    assert tk % tb == 0
    ratio = tk // tb
    e_lo = (-(MAX_DISTANCE - 1) - tk) // tb
    e_hi = -((-(MAX_DISTANCE - 1) - tb) // tb)
    spans = [-(-(j + n_sub - 1 + e_hi) // ratio) - 1 - (j + e_lo) // ratio
             for j in range(0, ratio * n_sub, n_sub)]
    return e_lo, e_hi, max(spans) + max(spans) % 2


def _diff_kernel(rb_ref, lam_ref, gs_ref, q_ref, k_ref, v_ref, o_ref,
                 qz_ref, m_ref, acc_ref, s0_ref, s1_ref, mc0_ref, mc1_ref, bias_ref,
                 *, t, tk, lam_init):
    h = pl.program_id(0)
    bi = pl.program_id(1)
    i = pl.program_id(2)
    n = k_ref.shape[1]
    nc = n // tk
    tb = BLOCK_ROWS
    n_sub = t // tb
    ratio = tk // tb
    e_lo, e_hi, near_chunks = _bias_geometry(tb, tk, n_sub)
    shift_before = rb_ref[NUM_BUCKETS // 2 - 1, h] * LOG2E
    shift_after = rb_ref[NUM_BUCKETS - 1, h] * LOG2E

    @pl.when(jnp.logical_and(bi == 0, i == 0))
    def _():
        width = (e_hi - 1 - e_lo) * tb + tk
        bucket = _rel_bucket(e_lo * tb + lax.broadcasted_iota(jnp.int32, (8, width), 1))
        gen = jnp.zeros((8, width), F32)
        for bkt in range(NUM_BUCKETS):
            gen = jnp.where(bucket == bkt, rb_ref[bkt, h] * LOG2E, gen)
        sub = lax.broadcasted_iota(jnp.int32, (8, width), 0)
        rows8 = gen
        for r in range(1, 8):
            rows8 = jnp.where(sub == r, pltpu.roll(gen, r, 1), rows8)
        for r0 in range(0, tb, 8):
            blk = rows8 if r0 == 0 else pltpu.roll(rows8, r0, 1)
            for e in range(e_lo + 1, e_hi):
                off = (e - e_lo) * tb
                bias_ref[e - e_lo, r0:r0 + 8, :] = blk[:, off:off + tk]
        bias_ref[0] = jnp.full((tb, tk), shift_before, F32)
        bias_ref[e_hi - e_lo] = jnp.full((tb, tk), shift_after, F32)

    lane = lax.broadcasted_iota(jnp.int32, (t, LANES), 1)
    low = lane < HEAD_DIM
    q = q_ref[0]
    zero = jnp.zeros_like(q)
    qz_ref[0:t, :] = jnp.where(low, q, zero)
    qz_ref[t:2 * t, :] = jnp.where(low, zero, q)
    m_ref[...] = jnp.full(m_ref.shape, M_INIT, F32)
    acc_ref[...] = jnp.zeros(acc_ref.shape, F32)

    n_near = min(near_chunks, nc)
    near0 = jnp.clip(jnp.floor_divide(i * n_sub + e_lo, ratio) + 1, 0, nc - n_near)

    def tail_bias(c, rows):
        block = i * n_sub + (rows.start % t) // tb
        return bias_ref[jnp.clip(ratio * c - block, e_lo, e_hi) - e_lo]

    _flash_sweep(
        nc, tk, qz_ref, k_ref, v_ref,
        tuple(slice(r, r + BLOCK_ROWS) for r in range(0, 2 * t, BLOCK_ROWS)),
        ((s0_ref, mc0_ref), (s1_ref, mc1_ref)), m_ref, acc_ref,
        v_ones=jnp.ones((tk, LANES), BF16), n_tail=n_near,
        main_chunk=lambda pos: pos + jnp.where(pos >= near0, n_near, 0),
        main_shift=lambda c: jnp.where(c < near0, shift_before, shift_after),
        tail_chunk=lambda u: near0 + u,
        tail_bias=tail_bias)

    lv = lam_ref[...]
    lam = (jnp.exp(jnp.sum(lv[0:1, :] * lv[1:2, :], axis=-1, keepdims=True))
           - jnp.exp(jnp.sum(lv[2:3, :] * lv[3:4, :], axis=-1, keepdims=True)) + lam_init)
    o1 = acc_ref[0:t, 0:LANES] / acc_ref[0:t, LANES:2 * LANES]
    o2 = acc_ref[t:2 * t, 0:LANES] / acc_ref[t:2 * t, LANES:2 * LANES]
    o = o1 - lam * o2
    o_ref[0] = (_rms(o, gs_ref[...]) * (1.0 - lam_init)).astype(o_ref.dtype)


def _diff_attention(act, rel_bias, lam_rows, g_subln, lam_init, tk=1024):
    b, n, _ = act.shape
    t = 2 * BLOCK_ROWS
    if 2 * _bias_geometry(BLOCK_ROWS, tk, t // BLOCK_ROWS)[2] > n // tk:
        t = BLOCK_ROWS
    e_lo, e_hi, _ = _bias_geometry(BLOCK_ROWS, tk, t // BLOCK_ROWS)
    return pl.pallas_call(
        functools.partial(_diff_kernel, t=t, tk=tk, lam_init=lam_init),
        grid=(N_DIFF_HEADS, b, n // t),
        in_specs=[
            pl.BlockSpec(memory_space=pltpu.SMEM),
            pl.BlockSpec((MOD_ROWS, LANES), lambda h, bi, i: (0, 0)),
            pl.BlockSpec((1, LANES), lambda h, bi, i: (0, 0)),
            pl.BlockSpec((1, t, LANES), lambda h, bi, i: (bi, i, QD_BLK + h)),
            pl.BlockSpec((1, n, LANES), lambda h, bi, i: (bi, 0, KD_BLK + h),
                         pipeline_mode=pl.Buffered(1)),
            pl.BlockSpec((1, n, LANES), lambda h, bi, i: (bi, 0, VD_BLK + h),
                         pipeline_mode=pl.Buffered(1)),
        ],
        out_specs=pl.BlockSpec((1, t, LANES), lambda h, bi, i: (bi, i, h)),
        out_shape=jax.ShapeDtypeStruct((b, n, DIFF_COLS), BF16),
        scratch_shapes=_flash_scratch(2 * t, tk, 2 * LANES)
        + [pltpu.VMEM((e_hi - e_lo + 1, BLOCK_ROWS, tk), F32)],
        compiler_params=pltpu.CompilerParams(
            dimension_semantics=("arbitrary", "arbitrary", "arbitrary"),
            vmem_limit_bytes=VMEM_LIMIT),
        name="diff_attention",
    )(rel_bias, lam_rows, g_subln, act, act, act)


def _ffn_kernel(x_ref, oa_ref, od_ref, mod_ref, wout_ref, gpm_ref, gpf_ref, wgu_ref, wdn_ref,
                gpo_ref, y_ref):
    gt1 = mod_ref[0, 2:3, :]
    sh2 = mod_ref[0, 3:4, :]
    sc2 = mod_ref[0, 4:5, :]
    gt2 = mod_ref[0, 5:6, :]
    wa = oa_ref.shape[2]
    d_ff = wdn_ref.shape[0]
    tm = x_ref.shape[1]
    groups = [slice(r0, r0 + FFN_GROUP_ROWS) for r0 in range(0, tm, FFN_GROUP_ROWS)]
    mixes = [jnp.dot(oa_ref[0, rows, :], wout_ref[0:wa, :], preferred_element_type=F32)
             + jnp.dot(od_ref[0, rows, :], wout_ref[wa:, :], preferred_element_type=F32)
             for rows in groups]
    x1s, gus, fs = [], [], []
    for rows, mix in zip(groups, mixes):
        x1 = x_ref[0, rows, :] + gt1 * _rms(mix, gpm_ref[...])
        h = _rms(x1, gpf_ref[...]) * (1.0 + sc2) + sh2
        x1s.append(x1)
        gus.append(jnp.dot(h.astype(BF16), wgu_ref[...], preferred_element_type=F32))
    for gu in gus:
        gate = gu[:, :d_ff]
        up = gu[:, d_ff:]
        act = gate * (1.0 / (1.0 + jnp.exp(-gate))) * up
        fs.append(jnp.dot(act.astype(BF16), wdn_ref[...], preferred_element_type=F32))
    for rows, x1, f in zip(groups, x1s, fs):
        y_ref[0, rows, :] = x1 + gt2 * _rms(f, gpo_ref[...])


def _out_ffn(x, out_a, out_d, mod3, w_out_bf, g_post_mix, g_pre_ffn, w_gu_bf, w_down_bf,
             g_post_ffn, tm=512):
    b, n, d = x.shape
    const = lambda bi, i: (0, 0)
    once = pl.Buffered(1)
    return pl.pallas_call(
        _ffn_kernel,
        grid=(b, n // tm),
        in_specs=[
            pl.BlockSpec((1, tm, d), lambda bi, i: (bi, i, 0)),
            pl.BlockSpec((1, tm, out_a.shape[2]), lambda bi, i: (bi, i, 0)),
            pl.BlockSpec((1, tm, out_d.shape[2]), lambda bi, i: (bi, i, 0)),
            pl.BlockSpec((1, 6, d), lambda bi, i: (bi, 0, 0)),
            pl.BlockSpec(w_out_bf.shape, const, pipeline_mode=once),
            pl.BlockSpec((1, d), const),
            pl.BlockSpec((1, d), const),
            pl.BlockSpec(w_gu_bf.shape, const, pipeline_mode=once),
            pl.BlockSpec(w_down_bf.shape, const, pipeline_mode=once),
            pl.BlockSpec((1, d), const),
        ],
        out_specs=pl.BlockSpec((1, tm, d), lambda bi, i: (bi, i, 0)),
        out_shape=jax.ShapeDtypeStruct((b, n, d), F32),
        compiler_params=pltpu.CompilerParams(
            dimension_semantics=("arbitrary", "arbitrary"), vmem_limit_bytes=VMEM_LIMIT),
        name="out_ffn",
    )(x, out_a, out_d, mod3, w_out_bf, g_post_mix, g_pre_ffn, w_gu_bf, w_down_bf, g_post_ffn)


def _rope_tables(n):
    rows = n // GRID_W
    row = jnp.repeat(jnp.arange(rows), GRID_W).astype(F32)
    col = jnp.tile(jnp.arange(GRID_W), rows).astype(F32)
    half = HEAD_DIM // 2
    inv = ROPE_THETA ** (-jnp.arange(0, half, 2, dtype=F32) / half)
    ang_r = row[:, None] * inv[None, :]
    ang_c = col[:, None] * inv[None, :]
    ang = jnp.concatenate([ang_r, ang_r, ang_c, ang_c] * (LANES // HEAD_DIM), axis=-1)
    sign = jnp.tile(jnp.concatenate([-jnp.ones((16,), F32), jnp.ones((16,), F32)]),
                    LANES // 32)
    return jnp.cos(ang), jnp.sin(ang) * sign[None, :]


def kernel(x_prompt, x_sample, c_prompt, c_sample, rel_bias, w_ada, b_ada, g_pre_mix, w_in,
           g_q, g_k, lam_q1, lam_k1, lam_q2, lam_k2, g_subln, w_out, g_post_mix, g_pre_ffn,
           w_gu, w_down, g_post_ffn):
    depth = w_ada.shape[0]
    d = x_prompt.shape[-1]
    xs = [x_prompt, x_sample]
    cs = [c_prompt, c_sample]
    n_c = sum(c.shape[0] for c in cs)
    assert n_c <= MOD_ROWS
    c_all = jnp.concatenate(cs + [jnp.zeros((MOD_ROWS - n_c, d), F32)], axis=0)
    seg = (jnp.arange(LANES)[:, None] // HEAD_DIM
           == jnp.arange(LANES)[None, :] // HEAD_DIM).astype(BF16)
    tables = [_rope_tables(max(x.shape[1] for x in xs))] * len(xs)

    for l in range(depth):
        lam_init = 0.8 - 0.6 * math.exp(-0.3 * l)
        mod = _modulation(c_all, w_ada, b_ada[l], l).reshape(MOD_ROWS, 6, d)
        w_in_bf = w_in[l].astype(BF16)
        w_out_bf = w_out[l].astype(BF16)
        w_gu_bf = w_gu[l].astype(BF16)
        w_down_bf = w_down[l].astype(BF16)
        gq2 = jnp.tile(g_q[l], LANES // HEAD_DIM).reshape(1, LANES)
        gk2 = jnp.tile(g_k[l], LANES // HEAD_DIM).reshape(1, LANES)
        lam_rows = jnp.zeros((MOD_ROWS, LANES), F32).at[0:4, 0:HEAD_DIM].set(
            jnp.stack([lam_q1[l], lam_k1[l], lam_q2[l], lam_k2[l]]))
        new_xs = []
        row0 = 0
        for x, (cos2, sin2) in zip(xs, tables):
            mod3 = mod[row0:row0 + x.shape[0]]
            row0 += x.shape[0]
            act = _in_projection(x, mod3, g_pre_mix[l].reshape(1, d), w_in_bf, gq2, gk2,
                                 cos2, sin2, seg)
            out_a = _gqa_attention(act)
            out_d = _diff_attention(act, rel_bias, lam_rows, g_subln[l].reshape(1, LANES),
                                    lam_init)
            new_xs.append(_out_ffn(x, out_a, out_d, mod3, w_out_bf,
                                   g_post_mix[l].reshape(1, d), g_pre_ffn[l].reshape(1, d),
                                   w_gu_bf, w_down_bf, g_post_ffn[l].reshape(1, d)))
        xs = new_xs
    return tuple(xs)
```

```python
import functools
import math

import jax
import jax.numpy as jnp
from jax import lax
from jax.experimental import pallas as pl
from jax.experimental.pallas import tpu as pltpu

F32 = jnp.float32
BF16 = jnp.bfloat16

HEAD_DIM = 64
N_GQA_HEADS = 8
N_GQA_KV = 2
GQA_GROUP = N_GQA_HEADS // N_GQA_KV
N_DIFF_HEADS = 4
GRID_W = 64
NUM_BUCKETS = 32
MAX_DISTANCE = 128
ROPE_THETA = 10000.0
EPS = 1e-6
ATTN_SCALE = 1.0 / math.sqrt(HEAD_DIM)
LOG2E = 1.4426950408889634
Q_SCALE = ATTN_SCALE * LOG2E

LANES = 128
GQA_Q_COLS = N_GQA_HEADS * HEAD_DIM
GQA_KV_COLS = N_GQA_KV * HEAD_DIM
DIFF_COLS = N_DIFF_HEADS * 2 * HEAD_DIM

QA_BLK = 0
KA_BLK = QA_BLK + GQA_Q_COLS // LANES
VA_BLK = KA_BLK + N_GQA_KV
QD_BLK = VA_BLK + N_GQA_KV
KD_BLK = QD_BLK + N_DIFF_HEADS
VD_BLK = KD_BLK + N_DIFF_HEADS
ACT_COLS = (VD_BLK + N_DIFF_HEADS) * LANES

MOD_ROWS = 8
M_INIT = -0.5 * float(jnp.finfo(jnp.float32).max)

VMEM_LIMIT = 56 * 1024 * 1024
FLASH_UNROLL = 2
STRIP_ROWS = 32
BLOCK_ROWS = 512
FFN_GROUP_ROWS = 256


def _rms(x, g):
    return x * lax.rsqrt(jnp.mean(x * x, axis=-1, keepdims=True) + EPS) * g


def _mod_kernel(c_ref, w_ref, b_ref, o_ref):
    c = c_ref[...]
    a = c * (1.0 / (1.0 + jnp.exp(-c)))
    a_hi = a.astype(BF16)
    a_lo = (a - a_hi.astype(F32)).astype(BF16)
    w = w_ref[0]
    w_hi = w.astype(BF16)
    w_lo = (w - w_hi.astype(F32)).astype(BF16)
    acc = jnp.dot(a_hi, w_hi, preferred_element_type=F32)
    acc += jnp.dot(a_lo, w_hi, preferred_element_type=F32)
    acc += jnp.dot(a_hi, w_lo, preferred_element_type=F32)
    o_ref[...] = acc + b_ref[...]


def _modulation(c_all, w_ada, b_ada, layer):
    rows, d = c_all.shape
    n_out = w_ada.shape[2]
    tn = 1536
    return pl.pallas_call(
        _mod_kernel,
        grid=(n_out // tn,),
        in_specs=[
            pl.BlockSpec((rows, d), lambda j: (0, 0)),
            pl.BlockSpec((1, d, tn), lambda j: (layer, 0, j)),
            pl.BlockSpec((1, tn), lambda j: (0, j)),
        ],
        out_specs=pl.BlockSpec((rows, tn), lambda j: (0, j)),
        out_shape=jax.ShapeDtypeStruct((rows, n_out), F32),
        compiler_params=pltpu.CompilerParams(
            dimension_semantics=("arbitrary",), vmem_limit_bytes=VMEM_LIMIT),
        name="modulation",
    )(c_all, w_ada, b_ada.reshape(1, n_out))


def _inproj_kernel(x_ref, mod_ref, g_ref, w_ref, gq_ref, gk_ref, cos_ref, sin_ref, seg_ref,
                   o_ref):
    for r0 in range(0, x_ref.shape[1], FFN_GROUP_ROWS):
        _inproj_rows(slice(r0, r0 + FFN_GROUP_ROWS), x_ref, mod_ref, g_ref, w_ref, gq_ref,
                     gk_ref, cos_ref, sin_ref, seg_ref, o_ref)


def _inproj_rows(rows, x_ref, mod_ref, g_ref, w_ref, gq_ref, gk_ref, cos_ref, sin_ref, seg_ref,
                 o_ref):
    x = x_ref[0, rows, :]
    sh = mod_ref[0, 0:1, :]
    sc = mod_ref[0, 1:2, :]
    h = _rms(x, g_ref[...]) * (1.0 + sc) + sh
    proj = jnp.dot(h.astype(BF16), w_ref[...], preferred_element_type=F32)

    cos = cos_ref[rows, :]
    sin = sin_ref[rows, :]
    seg = seg_ref[...]
    lane = lax.broadcasted_iota(jnp.int32, (x.shape[0], LANES), 1)
    first_half = (lane & 16) == 0
    low = lane < HEAD_DIM

    def norm_rope(t, g):
        ss = jnp.dot((t * t).astype(BF16), seg, preferred_element_type=F32)
        tn = t * lax.rsqrt(ss * (1.0 / HEAD_DIM) + EPS) * g
        rot = jnp.where(first_half, pltpu.roll(tn, LANES - 16, 1), pltpu.roll(tn, 16, 1))
        return tn * cos + rot * sin

    def put(blk, val):
        o_ref[0, rows, blk * LANES:(blk + 1) * LANES] = val.astype(BF16)

    col = 0
    for j in range(GQA_Q_COLS // LANES):
        put(QA_BLK + j, norm_rope(proj[:, col:col + LANES], gq_ref[...]) * Q_SCALE)
        col += LANES
    kr = norm_rope(proj[:, col:col + LANES], gk_ref[...])
    col += LANES
    kr_sw = pltpu.roll(kr, HEAD_DIM, 1)
    put(KA_BLK + 0, jnp.where(low, kr, kr_sw))
    put(KA_BLK + 1, jnp.where(low, kr_sw, kr))
    va = proj[:, col:col + LANES]
    col += LANES
    va_sw = pltpu.roll(va, HEAD_DIM, 1)
    put(VA_BLK + 0, jnp.where(low, va, 1.0))
    put(VA_BLK + 1, jnp.where(low, va_sw, 1.0))
    for j in range(N_DIFF_HEADS):
        put(QD_BLK + j, proj[:, col:col + LANES] * Q_SCALE)
        col += LANES
    for j in range(N_DIFF_HEADS):
        put(KD_BLK + j, proj[:, col:col + LANES])
        col += LANES
    for j in range(N_DIFF_HEADS):
        put(VD_BLK + j, proj[:, col:col + LANES])
        col += LANES


def _in_projection(x, mod3, g_pre, w_in_bf, gq2, gk2, cos2, sin2, seg, tm=512):
    b, n, d = x.shape
    in_w = w_in_bf.shape[1]
    const = lambda bi, i: (0, 0)
    return pl.pallas_call(
        _inproj_kernel,
        grid=(b, n // tm),
        in_specs=[
            pl.BlockSpec((1, tm, d), lambda bi, i: (bi, i, 0)),
            pl.BlockSpec((1, 6, d), lambda bi, i: (bi, 0, 0)),
            pl.BlockSpec((1, d), const),
            pl.BlockSpec((d, in_w), const),
            pl.BlockSpec((1, LANES), const),
            pl.BlockSpec((1, LANES), const),
            pl.BlockSpec((tm, LANES), lambda bi, i: (i, 0)),
            pl.BlockSpec((tm, LANES), lambda bi, i: (i, 0)),
            pl.BlockSpec((LANES, LANES), const),
        ],
        out_specs=pl.BlockSpec((1, tm, ACT_COLS), lambda bi, i: (bi, i, 0)),
        out_shape=jax.ShapeDtypeStruct((b, n, ACT_COLS), BF16),
        compiler_params=pltpu.CompilerParams(
            dimension_semantics=("arbitrary", "arbitrary"), vmem_limit_bytes=VMEM_LIMIT),
        name="in_projection",
    )(x, mod3, g_pre, w_in_bf, gq2, gk2, cos2, sin2, seg)


def _flash_sweep(nc, tk, qz_ref, k_ref, v_ref, blocks, bufs, m_ref, acc_ref, *, v_ones=None,
                 n_tail=0, main_chunk=None, main_shift=None, tail_chunk=None, tail_bias=None):
    n_main = nc - n_tail
    unroll = min(FLASH_UNROLL, n_main) if n_main else 0
    assert n_tail % 2 == 0 and (n_main == 0 or (unroll % 2 == 0 and n_main % unroll == 0))

    def chunk_of(kind, idx):
        if kind == "tail":
            return tail_chunk(idx), None, tail_bias
        c = idx if main_chunk is None else main_chunk(idx)
        return c, (None if main_shift is None else main_shift(c)), None

    def scores(kind, idx, s_ref, mc_ref):
        c, shift, bias = chunk_of(kind, idx)
        start = pl.multiple_of(c * tk, tk)
        kc = k_ref[0, pl.ds(start, tk), :]
        for rows in blocks:
            s = lax.dot_general(qz_ref[rows, :], kc, (((1,), (1,)), ((), ())),
                                preferred_element_type=F32)
            if bias is not None:
                s = s + bias(c, rows)
            s_ref[rows, :] = s
            mc = jnp.max(s, axis=-1, keepdims=True)
            if shift is not None:
                mc = mc + shift
            mc_ref[rows, :] = jnp.broadcast_to(mc, (s.shape[0], LANES))

    def values(kind, idx, s_ref, mc_ref):
        c, shift, _ = chunk_of(kind, idx)
        start = pl.multiple_of(c * tk, tk)
        vc = v_ref[0, pl.ds(start, tk), :]
        if v_ones is not None:
            vc = jnp.concatenate([vc, v_ones], axis=1)
        reps = acc_ref.shape[1] // LANES
        for rows in blocks:
            strips = []
            for r0 in range(rows.start, rows.stop, STRIP_ROWS):
                rs = slice(r0, r0 + STRIP_ROWS)
                m_prev = m_ref[rs, :]
                m_new = jnp.maximum(m_prev, mc_ref[rs, :])
                alpha = jnp.exp2(m_prev - m_new)
                m_ref[rs, :] = m_new
                for j in range(reps):
                    cols = slice(j * LANES, (j + 1) * LANES)
                    acc_ref[rs, cols] = alpha * acc_ref[rs, cols]
                m_sub = m_new if shift is None else m_new - shift
                strips.append(jnp.concatenate(
                    [jnp.exp2(s_ref[rs, j * LANES:(j + 1) * LANES] - m_sub).astype(BF16)
                     for j in range(tk // LANES)], axis=1))
            p = jnp.concatenate(strips, axis=0)
            acc_ref[rows, :] += jnp.dot(p, vc, preferred_element_type=F32)

    def group(kind, base, count, nxt):
        for u in range(count):
            if u + 1 < count:
                scores(kind, base + u + 1, *bufs[(u + 1) % 2])
            elif nxt is not None:
                scores(*nxt, *bufs[(u + 1) % 2])
            values(kind, base + u, *bufs[u % 2])

    tail_start = ("tail", 0) if n_tail else None
    scores(*(("main", 0) if n_main else tail_start), *bufs[0])
    if n_main:
        n_groups = n_main // unroll

        def body(g, carry):
            group("main", g * unroll, unroll, ("main", (g + 1) * unroll))
            return carry

        lax.fori_loop(0, n_groups - 1, body, 0)
        group("main", n_main - unroll, unroll, tail_start)
    if n_tail:
        group("tail", 0, n_tail, None)


def _flash_scratch(mrows, tk, acc_cols):
    return [
        pltpu.VMEM((mrows, LANES), BF16),
        pltpu.VMEM((mrows, LANES), F32),
        pltpu.VMEM((mrows, acc_cols), F32),
        pltpu.VMEM((mrows, tk), F32),
        pltpu.VMEM((mrows, tk), F32),
        pltpu.VMEM((mrows, LANES), F32),
        pltpu.VMEM((mrows, LANES), F32),
    ]


def _gqa_kernel(q_ref, k_ref, v_ref, o_ref, qz_ref, m_ref, acc_ref, s0_ref, s1_ref,
                mc0_ref, mc1_ref, *, tq, tk):
    n = k_ref.shape[1]
    mrows = GQA_GROUP * tq
    lane = lax.broadcasted_iota(jnp.int32, (tq, LANES), 1)
    low = lane < HEAD_DIM
    for g in range(GQA_GROUP):
        slab = q_ref[0, :, (g // 2) * LANES:(g // 2 + 1) * LANES]
        keep = low if g % 2 == 0 else jnp.logical_not(low)
        qz_ref[g * tq:(g + 1) * tq, :] = jnp.where(keep, slab, jnp.zeros_like(slab))
    m_ref[...] = jnp.full(m_ref.shape, M_INIT, F32)
    acc_ref[...] = jnp.zeros(acc_ref.shape, F32)

    blocks = tuple(slice(r, r + BLOCK_ROWS) for r in range(0, mrows, BLOCK_ROWS))
    _flash_sweep(n // tk, tk, qz_ref, k_ref, v_ref, blocks,
                 ((s0_ref, mc0_ref), (s1_ref, mc1_ref)), m_ref, acc_ref)

    outs = []
    for g in range(GQA_GROUP):
        a = acc_ref[g * tq:(g + 1) * tq, :]
        outs.append(a / pltpu.roll(a, HEAD_DIM, 1))
    for j in range(GQA_GROUP // 2):
        pair = jnp.where(low, outs[2 * j], pltpu.roll(outs[2 * j + 1], HEAD_DIM, 1))
        o_ref[0, :, j * LANES:(j + 1) * LANES] = pair.astype(o_ref.dtype)


def _gqa_attention(act, tq=512, tk=1024):
    b, n, _ = act.shape
    width = GQA_GROUP * HEAD_DIM
    mrows = GQA_GROUP * tq
    return pl.pallas_call(
        functools.partial(_gqa_kernel, tq=tq, tk=tk),
        grid=(b, N_GQA_KV, n // tq),
        in_specs=[
            pl.BlockSpec((1, tq, width), lambda bi, h, i: (bi, i, h)),
            pl.BlockSpec((1, n, LANES), lambda bi, h, i: (bi, 0, KA_BLK + h)),
            pl.BlockSpec((1, n, LANES), lambda bi, h, i: (bi, 0, VA_BLK + h)),
        ],
        out_specs=pl.BlockSpec((1, tq, width), lambda bi, h, i: (bi, i, h)),
        out_shape=jax.ShapeDtypeStruct((b, n, GQA_Q_COLS), BF16),
        scratch_shapes=_flash_scratch(mrows, tk, LANES),
        compiler_params=pltpu.CompilerParams(
            dimension_semantics=("arbitrary", "arbitrary", "arbitrary"),
            vmem_limit_bytes=VMEM_LIMIT),
        name="gqa_attention",
    )(act, act, act)


def _rel_bucket(rel):
    half = NUM_BUCKETS // 2
    max_exact = half // 2
    n = jnp.minimum(jnp.abs(rel), MAX_DISTANCE)
    n2 = n * n
    large = jnp.full(rel.shape, max_exact, jnp.int32)
    for k in range(1, half - max_exact):
        large = large + (n2 >= (max_exact * max_exact) * (2 ** k)).astype(jnp.int32)
    return jnp.where(rel > 0, half, 0) + jnp.where(n < max_exact, n, large)


def _bias_geometry(tb, tk, n_sub):
    assert tk % tb == 0
    ratio = tk // tb
    e_lo = (-(MAX_DISTANCE - 1) - tk) // tb
    e_hi = -((-(MAX_DISTANCE - 1) - tb) // tb)
    spans = [-(-(j + n_sub - 1 + e_hi) // ratio) - 1 - (j + e_lo) // ratio
             for j in range(0, ratio * n_sub, n_sub)]
    return e_lo, e_hi, max(spans) + max(spans) % 2


def _diff_kernel(rb_ref, lam_ref, gs_ref, q_ref, k_ref, v_ref, o_ref,
                 qz_ref, m_ref, acc_ref, s0_ref, s1_ref, mc0_ref, mc1_ref, bias_ref,
                 *, t, tk, lam_init):
    h = pl.program_id(0)
    bi = pl.program_id(1)
    i = pl.program_id(2)
    n = k_ref.shape[1]
    nc = n // tk
    tb = BLOCK_ROWS
    n_sub = t // tb
    ratio = tk // tb
    e_lo, e_hi, near_chunks = _bias_geometry(tb, tk, n_sub)
    shift_before = rb_ref[NUM_BUCKETS // 2 - 1, h] * LOG2E
    shift_after = rb_ref[NUM_BUCKETS - 1, h] * LOG2E

    @pl.when(jnp.logical_and(bi == 0, i == 0))
    def _():
        width = (e_hi - 1 - e_lo) * tb + tk
        bucket = _rel_bucket(e_lo * tb + lax.broadcasted_iota(jnp.int32, (8, width), 1))
        gen = jnp.zeros((8, width), F32)
        for bkt in range(NUM_BUCKETS):
            gen = jnp.where(bucket == bkt, rb_ref[bkt, h] * LOG2E, gen)
        sub = lax.broadcasted_iota(jnp.int32, (8, width), 0)
        rows8 = gen
        for r in range(1, 8):
            rows8 = jnp.where(sub == r, pltpu.roll(gen, r, 1), rows8)
        for r0 in range(0, tb, 8):
            blk = rows8 if r0 == 0 else pltpu.roll(rows8, r0, 1)
            for e in range(e_lo + 1, e_hi):
                off = (e - e_lo) * tb
                bias_ref[e - e_lo, r0:r0 + 8, :] = blk[:, off:off + tk]
        bias_ref[0] = jnp.full((tb, tk), shift_before, F32)
        bias_ref[e_hi - e_lo] = jnp.full((tb, tk), shift_after, F32)

    lane = lax.broadcasted_iota(jnp.int32, (t, LANES), 1)
    low = lane < HEAD_DIM
    q = q_ref[0]
    zero = jnp.zeros_like(q)
    qz_ref[0:t, :] = jnp.where(low, q, zero)
    qz_ref[t:2 * t, :] = jnp.where(low, zero, q)
    m_ref[...] = jnp.full(m_ref.shape, M_INIT, F32)
    acc_ref[...] = jnp.zeros(acc_ref.shape, F32)

    n_near = min(near_chunks, nc)
    near0 = jnp.clip(jnp.floor_divide(i * n_sub + e_lo, ratio) + 1, 0, nc - n_near)

    def tail_bias(c, rows):
        block = i * n_sub + (rows.start % t) // tb
        return bias_ref[jnp.clip(ratio * c - block, e_lo, e_hi) - e_lo]

    _flash_sweep(
        nc, tk, qz_ref, k_ref, v_ref,
        tuple(slice(r, r + BLOCK_ROWS) for r in range(0, 2 * t, BLOCK_ROWS)),
        ((s0_ref, mc0_ref), (s1_ref, mc1_ref)), m_ref, acc_ref,
        v_ones=jnp.ones((tk, LANES), BF16), n_tail=n_near,
        main_chunk=lambda pos: pos + jnp.where(pos >= near0, n_near, 0),
        main_shift=lambda c: jnp.where(c < near0, shift_before, shift_after),
        tail_chunk=lambda u: near0 + u,
        tail_bias=tail_bias)

    lv = lam_ref[...]
    lam = (jnp.exp(jnp.sum(lv[0:1, :] * lv[1:2, :], axis=-1, keepdims=True))
           - jnp.exp(jnp.sum(lv[2:3, :] * lv[3:4, :], axis=-1, keepdims=True)) + lam_init)
    o1 = acc_ref[0:t, 0:LANES] / acc_ref[0:t, LANES:2 * LANES]
    o2 = acc_ref[t:2 * t, 0:LANES] / acc_ref[t:2 * t, LANES:2 * LANES]
    o = o1 - lam * o2
    o_ref[0] = (_rms(o, gs_ref[...]) * (1.0 - lam_init)).astype(o_ref.dtype)


def _diff_attention(act, rel_bias, lam_rows, g_subln, lam_init, t=1024, tk=1024):
    b, n, _ = act.shape
    e_lo, e_hi, _ = _bias_geometry(BLOCK_ROWS, tk, t // BLOCK_ROWS)
    return pl.pallas_call(
        functools.partial(_diff_kernel, t=t, tk=tk, lam_init=lam_init),
        grid=(N_DIFF_HEADS, b, n // t),
        in_specs=[
            pl.BlockSpec(memory_space=pltpu.SMEM),
            pl.BlockSpec((MOD_ROWS, LANES), lambda h, bi, i: (0, 0)),
            pl.BlockSpec((1, LANES), lambda h, bi, i: (0, 0)),
            pl.BlockSpec((1, t, LANES), lambda h, bi, i: (bi, i, QD_BLK + h)),
            pl.BlockSpec((1, n, LANES), lambda h, bi, i: (bi, 0, KD_BLK + h),
                         pipeline_mode=pl.Buffered(1)),
            pl.BlockSpec((1, n, LANES), lambda h, bi, i: (bi, 0, VD_BLK + h),
                         pipeline_mode=pl.Buffered(1)),
        ],
        out_specs=pl.BlockSpec((1, t, LANES), lambda h, bi, i: (bi, i, h)),
        out_shape=jax.ShapeDtypeStruct((b, n, DIFF_COLS), BF16),
        scratch_shapes=_flash_scratch(2 * t, tk, 2 * LANES)
        + [pltpu.VMEM((e_hi - e_lo + 1, BLOCK_ROWS, tk), F32)],
        compiler_params=pltpu.CompilerParams(
            dimension_semantics=("arbitrary", "arbitrary", "arbitrary"),
            vmem_limit_bytes=VMEM_LIMIT),
        name="diff_attention",
    )(rel_bias, lam_rows, g_subln, act, act, act)


def _ffn_kernel(x_ref, oa_ref, od_ref, mod_ref, wout_ref, gpm_ref, gpf_ref, wgu_ref, wdn_ref,
                gpo_ref, y_ref):
    gt1 = mod_ref[0, 2:3, :]
    sh2 = mod_ref[0, 3:4, :]
    sc2 = mod_ref[0, 4:5, :]
    gt2 = mod_ref[0, 5:6, :]
    wa = oa_ref.shape[2]
    d_ff = wdn_ref.shape[0]
    tm = x_ref.shape[1]
    groups = [slice(r0, r0 + FFN_GROUP_ROWS) for r0 in range(0, tm, FFN_GROUP_ROWS)]
    mixes = [jnp.dot(oa_ref[0, rows, :], wout_ref[0:wa, :], preferred_element_type=F32)
             + jnp.dot(od_ref[0, rows, :], wout_ref[wa:, :], preferred_element_type=F32)
             for rows in groups]
    x1s, gus, fs = [], [], []
    for rows, mix in zip(groups, mixes):
        x1 = x_ref[0, rows, :] + gt1 * _rms(mix, gpm_ref[...])
        h = _rms(x1, gpf_ref[...]) * (1.0 + sc2) + sh2
        x1s.append(x1)
        gus.append(jnp.dot(h.astype(BF16), wgu_ref[...], preferred_element_type=F32))
    for gu in gus:
        gate = gu[:, :d_ff]
        up = gu[:, d_ff:]
        act = gate * (1.0 / (1.0 + jnp.exp(-gate))) * up
        fs.append(jnp.dot(act.astype(BF16), wdn_ref[...], preferred_element_type=F32))
    for rows, x1, f in zip(groups, x1s, fs):
        y_ref[0, rows, :] = x1 + gt2 * _rms(f, gpo_ref[...])


def _out_ffn(x, out_a, out_d, mod3, w_out_bf, g_post_mix, g_pre_ffn, w_gu_bf, w_down_bf,
             g_post_ffn, tm=512):
    b, n, d = x.shape
    const = lambda bi, i: (0, 0)
    once = pl.Buffered(1)
    return pl.pallas_call(
        _ffn_kernel,
        grid=(b, n // tm),
        in_specs=[
            pl.BlockSpec((1, tm, d), lambda bi, i: (bi, i, 0)),
            pl.BlockSpec((1, tm, out_a.shape[2]), lambda bi, i: (bi, i, 0)),
            pl.BlockSpec((1, tm, out_d.shape[2]), lambda bi, i: (bi, i, 0)),
            pl.BlockSpec((1, 6, d), lambda bi, i: (bi, 0, 0)),
            pl.BlockSpec(w_out_bf.shape, const, pipeline_mode=once),
            pl.BlockSpec((1, d), const),
            pl.BlockSpec((1, d), const),
            pl.BlockSpec(w_gu_bf.shape, const, pipeline_mode=once),
            pl.BlockSpec(w_down_bf.shape, const, pipeline_mode=once),
            pl.BlockSpec((1, d), const),
        ],
        out_specs=pl.BlockSpec((1, tm, d), lambda bi, i: (bi, i, 0)),
        out_shape=jax.ShapeDtypeStruct((b, n, d), F32),
        compiler_params=pltpu.CompilerParams(
            dimension_semantics=("arbitrary", "arbitrary"), vmem_limit_bytes=VMEM_LIMIT),
        name="out_ffn",
    )(x, out_a, out_d, mod3, w_out_bf, g_post_mix, g_pre_ffn, w_gu_bf, w_down_bf, g_post_ffn)


def _rope_tables(n):
    pos = jnp.arange(n)
    row = (pos // GRID_W).astype(F32)
    col = (pos % GRID_W).astype(F32)
    half = HEAD_DIM // 2
    quarter = half // 2
    inv = ROPE_THETA ** (-jnp.arange(0, half, 2, dtype=F32) / half)
    lane = jnp.arange(LANES)
    inv_lane = inv[lane % quarter]
    is_row = (lane % HEAD_DIM < half)[None, :]
    ang = jnp.where(is_row, row[:, None], col[:, None]) * inv_lane[None, :]
    sign = jnp.where(lane % half < quarter, -1.0, 1.0).astype(F32)
    return jnp.cos(ang), jnp.sin(ang) * sign[None, :]


def kernel(x_prompt, x_sample, c_prompt, c_sample, rel_bias, w_ada, b_ada, g_pre_mix, w_in,
           g_q, g_k, lam_q1, lam_k1, lam_q2, lam_k2, g_subln, w_out, g_post_mix, g_pre_ffn,
           w_gu, w_down, g_post_ffn):
    depth = w_ada.shape[0]
    d = x_prompt.shape[-1]
    xs = [x_prompt, x_sample]
    cs = [c_prompt, c_sample]
    n_c = sum(c.shape[0] for c in cs)
    assert n_c <= MOD_ROWS
    c_all = jnp.concatenate(cs + [jnp.zeros((MOD_ROWS - n_c, d), F32)], axis=0)
    seg = (jnp.arange(LANES)[:, None] // HEAD_DIM
           == jnp.arange(LANES)[None, :] // HEAD_DIM).astype(BF16)
    tables = [_rope_tables(max(x.shape[1] for x in xs))] * len(xs)

    for l in range(depth):
        lam_init = 0.8 - 0.6 * math.exp(-0.3 * l)
        mod = _modulation(c_all, w_ada, b_ada[l], l).reshape(MOD_ROWS, 6, d)
        w_in_bf = w_in[l].astype(BF16)
        w_out_bf = w_out[l].astype(BF16)
        w_gu_bf = w_gu[l].astype(BF16)
        w_down_bf = w_down[l].astype(BF16)
        gq2 = jnp.tile(g_q[l], LANES // HEAD_DIM).reshape(1, LANES)
        gk2 = jnp.tile(g_k[l], LANES // HEAD_DIM).reshape(1, LANES)
        lam_rows = jnp.zeros((MOD_ROWS, LANES), F32).at[0:4, 0:HEAD_DIM].set(
            jnp.stack([lam_q1[l], lam_k1[l], lam_q2[l], lam_k2[l]]))
        new_xs = []
        row0 = 0
        for x, (cos2, sin2) in zip(xs, tables):
            mod3 = mod[row0:row0 + x.shape[0]]
            row0 += x.shape[0]
            act = _in_projection(x, mod3, g_pre_mix[l].reshape(1, d), w_in_bf, gq2, gk2,
                                 cos2, sin2, seg)
            out_a = _gqa_attention(act)
            out_d = _diff_attention(act, rel_bias, lam_rows, g_subln[l].reshape(1, LANES),
                                    lam_init)
            new_xs.append(_out_ffn(x, out_a, out_d, mod3, w_out_bf,
                                   g_post_mix[l].reshape(1, d), g_pre_ffn[l].reshape(1, d),
                                   w_gu_bf, w_down_bf, g_post_ffn[l].reshape(1, d)))
        xs = new_xs
    return tuple(xs)
```

```python
import functools
import math

import jax
import jax.numpy as jnp
from jax import lax
from jax.experimental import pallas as pl
from jax.experimental.pallas import tpu as pltpu

F32 = jnp.float32
BF16 = jnp.bfloat16

HEAD_DIM = 64
N_GQA_HEADS = 8
N_GQA_KV = 2
GQA_GROUP = N_GQA_HEADS // N_GQA_KV
N_DIFF_HEADS = 4
GRID_W = 64
NUM_BUCKETS = 32
MAX_DISTANCE = 128
ROPE_THETA = 10000.0
EPS = 1e-6
ATTN_SCALE = 1.0 / math.sqrt(HEAD_DIM)
LOG2E = 1.4426950408889634
Q_SCALE = ATTN_SCALE * LOG2E

LANES = 128
GQA_Q_COLS = N_GQA_HEADS * HEAD_DIM
GQA_KV_COLS = N_GQA_KV * HEAD_DIM
DIFF_COLS = N_DIFF_HEADS * 2 * HEAD_DIM

QA_BLK = 0
KA_BLK = QA_BLK + GQA_Q_COLS // LANES
VA_BLK = KA_BLK + N_GQA_KV
QD_BLK = VA_BLK + N_GQA_KV
KD_BLK = QD_BLK + N_DIFF_HEADS
VD_BLK = KD_BLK + N_DIFF_HEADS
ACT_COLS = (VD_BLK + N_DIFF_HEADS) * LANES

MOD_ROWS = 8
M_INIT = -0.5 * float(jnp.finfo(jnp.float32).max)

VMEM_LIMIT = 56 * 1024 * 1024
FLASH_UNROLL = 2
STRIP_ROWS = 32
BLOCK_ROWS = 512
FFN_GROUP_ROWS = 256


def _rms(x, g):
    return x * lax.rsqrt(jnp.mean(x * x, axis=-1, keepdims=True) + EPS) * g


def _mod_kernel(c_ref, w_ref, b_ref, o_ref):
    c = c_ref[...]
    a = c * (1.0 / (1.0 + jnp.exp(-c)))
    a_hi = a.astype(BF16)
    a_lo = (a - a_hi.astype(F32)).astype(BF16)
    w = w_ref[0]
    w_hi = w.astype(BF16)
    w_lo = (w - w_hi.astype(F32)).astype(BF16)
    acc = jnp.dot(a_hi, w_hi, preferred_element_type=F32)
    acc += jnp.dot(a_lo, w_hi, preferred_element_type=F32)
    acc += jnp.dot(a_hi, w_lo, preferred_element_type=F32)
    o_ref[...] = acc + b_ref[...]


def _modulation(c_all, w_ada, b_ada, layer):
    rows, d = c_all.shape
    n_out = w_ada.shape[2]
    tn = 1536
    return pl.pallas_call(
        _mod_kernel,
        grid=(n_out // tn,),
        in_specs=[
            pl.BlockSpec((rows, d), lambda j: (0, 0)),
            pl.BlockSpec((1, d, tn), lambda j: (layer, 0, j)),
            pl.BlockSpec((1, tn), lambda j: (0, j)),
        ],
        out_specs=pl.BlockSpec((rows, tn), lambda j: (0, j)),
        out_shape=jax.ShapeDtypeStruct((rows, n_out), F32),
        compiler_params=pltpu.CompilerParams(
            dimension_semantics=("arbitrary",), vmem_limit_bytes=VMEM_LIMIT),
        name="modulation",
    )(c_all, w_ada, b_ada.reshape(1, n_out))


def _inproj_kernel(x_ref, mod_ref, g_ref, w_ref, gq_ref, gk_ref, cos_ref, sin_ref, seg_ref,
                   o_ref):
    for r0 in range(0, x_ref.shape[1], FFN_GROUP_ROWS):
        _inproj_rows(slice(r0, r0 + FFN_GROUP_ROWS), x_ref, mod_ref, g_ref, w_ref, gq_ref,
                     gk_ref, cos_ref, sin_ref, seg_ref, o_ref)


def _inproj_rows(rows, x_ref, mod_ref, g_ref, w_ref, gq_ref, gk_ref, cos_ref, sin_ref, seg_ref,
                 o_ref):
    x = x_ref[0, rows, :]
    sh = mod_ref[0, 0:1, :]
    sc = mod_ref[0, 1:2, :]
    h = _rms(x, g_ref[...]) * (1.0 + sc) + sh
    proj = jnp.dot(h.astype(BF16), w_ref[...], preferred_element_type=F32)

    cos = cos_ref[rows, :]
    sin = sin_ref[rows, :]
    seg = seg_ref[...]
    lane = lax.broadcasted_iota(jnp.int32, (x.shape[0], LANES), 1)
    first_half = (lane & 16) == 0
    low = lane < HEAD_DIM

    def norm_rope(t, g):
        ss = jnp.dot((t * t).astype(BF16), seg, preferred_element_type=F32)
        tn = t * lax.rsqrt(ss * (1.0 / HEAD_DIM) + EPS) * g
        rot = jnp.where(first_half, pltpu.roll(tn, LANES - 16, 1), pltpu.roll(tn, 16, 1))
        return tn * cos + rot * sin

    def put(blk, val):
        o_ref[0, rows, blk * LANES:(blk + 1) * LANES] = val.astype(BF16)

    col = 0
    for j in range(GQA_Q_COLS // LANES):
        put(QA_BLK + j, norm_rope(proj[:, col:col + LANES], gq_ref[...]) * Q_SCALE)
        col += LANES
    kr = norm_rope(proj[:, col:col + LANES], gk_ref[...])
    col += LANES
    kr_sw = pltpu.roll(kr, HEAD_DIM, 1)
    put(KA_BLK + 0, jnp.where(low, kr, kr_sw))
    put(KA_BLK + 1, jnp.where(low, kr_sw, kr))
    va = proj[:, col:col + LANES]
    col += LANES
    va_sw = pltpu.roll(va, HEAD_DIM, 1)
    put(VA_BLK + 0, jnp.where(low, va, 1.0))
    put(VA_BLK + 1, jnp.where(low, va_sw, 1.0))
    for j in range(N_DIFF_HEADS):
        put(QD_BLK + j, proj[:, col:col + LANES] * Q_SCALE)
        col += LANES
    for j in range(N_DIFF_HEADS):
        put(KD_BLK + j, proj[:, col:col + LANES])
        col += LANES
    for j in range(N_DIFF_HEADS):
        put(VD_BLK + j, proj[:, col:col + LANES])
        col += LANES


def _in_projection(x, mod3, g_pre, w_in_bf, gq2, gk2, cos2, sin2, seg, tm=512):
    b, n, d = x.shape
    in_w = w_in_bf.shape[1]
    const = lambda bi, i: (0, 0)
    return pl.pallas_call(
        _inproj_kernel,
        grid=(b, n // tm),
        in_specs=[
            pl.BlockSpec((1, tm, d), lambda bi, i: (bi, i, 0)),
            pl.BlockSpec((1, 6, d), lambda bi, i: (bi, 0, 0)),
            pl.BlockSpec((1, d), const),
            pl.BlockSpec((d, in_w), const),
            pl.BlockSpec((1, LANES), const),
            pl.BlockSpec((1, LANES), const),
            pl.BlockSpec((tm, LANES), lambda bi, i: (i, 0)),
            pl.BlockSpec((tm, LANES), lambda bi, i: (i, 0)),
            pl.BlockSpec((LANES, LANES), const),
        ],
        out_specs=pl.BlockSpec((1, tm, ACT_COLS), lambda bi, i: (bi, i, 0)),
        out_shape=jax.ShapeDtypeStruct((b, n, ACT_COLS), BF16),
        compiler_params=pltpu.CompilerParams(
            dimension_semantics=("arbitrary", "arbitrary"), vmem_limit_bytes=VMEM_LIMIT),
        name="in_projection",
    )(x, mod3, g_pre, w_in_bf, gq2, gk2, cos2, sin2, seg)


def _flash_sweep(nc, tk, qz_ref, k_ref, v_ref, blocks, bufs, m_ref, acc_ref, *, v_ones=None,
                 n_tail=0, main_chunk=None, main_shift=None, tail_chunk=None, tail_bias=None):
    n_main = nc - n_tail
    unroll = min(FLASH_UNROLL, n_main) if n_main else 0
    assert n_tail % 2 == 0 and (n_main == 0 or (unroll % 2 == 0 and n_main % unroll == 0))

    def chunk_of(kind, idx):
        if kind == "tail":
            return tail_chunk(idx), None, tail_bias
        c = idx if main_chunk is None else main_chunk(idx)
        return c, (None if main_shift is None else main_shift(c)), None

    def scores(kind, idx, s_ref, mc_ref):
        c, shift, bias = chunk_of(kind, idx)
        start = pl.multiple_of(c * tk, tk)
        kc = k_ref[0, pl.ds(start, tk), :]
        for rows in blocks:
            s = lax.dot_general(qz_ref[rows, :], kc, (((1,), (1,)), ((), ())),
                                preferred_element_type=F32)
            if bias is not None:
                s = s + bias(c, rows)
            s_ref[rows, :] = s
            mc = jnp.max(s, axis=-1, keepdims=True)
            if shift is not None:
                mc = mc + shift
            mc_ref[rows, :] = jnp.broadcast_to(mc, (s.shape[0], LANES))

    def values(kind, idx, s_ref, mc_ref):
        c, shift, _ = chunk_of(kind, idx)
        start = pl.multiple_of(c * tk, tk)
        vc = v_ref[0, pl.ds(start, tk), :]
        if v_ones is not None:
            vc = jnp.concatenate([vc, v_ones], axis=1)
        acc_refs = acc_ref if isinstance(acc_ref, tuple) else (acc_ref,)
        for rows in blocks:
            strips = []
            for r0 in range(rows.start, rows.stop, STRIP_ROWS):
                rs = slice(r0, r0 + STRIP_ROWS)
                m_prev = m_ref[rs, :]
                m_new = jnp.maximum(m_prev, mc_ref[rs, :])
                alpha = jnp.exp2(m_prev - m_new)
                m_ref[rs, :] = m_new
                for a_ref in acc_refs:
                    a_ref[rs, :] = alpha * a_ref[rs, :]
                m_sub = m_new if shift is None else m_new - shift
                strips.append(jnp.concatenate(
                    [jnp.exp2(s_ref[rs, j * LANES:(j + 1) * LANES] - m_sub).astype(BF16)
                     for j in range(tk // LANES)], axis=1))
            p = jnp.concatenate(strips, axis=0)
            pv = jnp.dot(p, vc, preferred_element_type=F32)
            for j, a_ref in enumerate(acc_refs):
                a_ref[rows, :] += pv[:, j * LANES:(j + 1) * LANES]

    def group(kind, base, count, nxt):
        for u in range(count):
            if u + 1 < count:
                scores(kind, base + u + 1, *bufs[(u + 1) % 2])
            elif nxt is not None:
                scores(*nxt, *bufs[(u + 1) % 2])
            values(kind, base + u, *bufs[u % 2])

    tail_start = ("tail", 0) if n_tail else None
    scores(*(("main", 0) if n_main else tail_start), *bufs[0])
    if n_main:
        n_groups = n_main // unroll

        def body(g, carry):
            group("main", g * unroll, unroll, ("main", (g + 1) * unroll))
            return carry

        lax.fori_loop(0, n_groups - 1, body, 0)
        group("main", n_main - unroll, unroll, tail_start)
    if n_tail:
        group("tail", 0, n_tail, None)


def _flash_scratch(mrows, tk, acc_cols):
    return [
        pltpu.VMEM((mrows, LANES), BF16),
        pltpu.VMEM((mrows, LANES), F32),
        pltpu.VMEM((mrows, acc_cols), F32),
        pltpu.VMEM((mrows, tk), F32),
        pltpu.VMEM((mrows, tk), F32),
        pltpu.VMEM((mrows, LANES), F32),
        pltpu.VMEM((mrows, LANES), F32),
    ]


def _gqa_kernel(q_ref, k_ref, v_ref, o_ref, qz_ref, m_ref, acc_ref, s0_ref, s1_ref,
                mc0_ref, mc1_ref, *, tq, tk):
    n = k_ref.shape[1]
    mrows = GQA_GROUP * tq
    lane = lax.broadcasted_iota(jnp.int32, (tq, LANES), 1)
    low = lane < HEAD_DIM
    for g in range(GQA_GROUP):
        slab = q_ref[0, :, (g // 2) * LANES:(g // 2 + 1) * LANES]
        keep = low if g % 2 == 0 else jnp.logical_not(low)
        qz_ref[g * tq:(g + 1) * tq, :] = jnp.where(keep, slab, jnp.zeros_like(slab))
    m_ref[...] = jnp.full(m_ref.shape, M_INIT, F32)
    acc_ref[...] = jnp.zeros(acc_ref.shape, F32)

    blocks = tuple(slice(r, r + BLOCK_ROWS) for r in range(0, mrows, BLOCK_ROWS))
    _flash_sweep(n // tk, tk, qz_ref, k_ref, v_ref, blocks,
                 ((s0_ref, mc0_ref), (s1_ref, mc1_ref)), m_ref, acc_ref)

    outs = []
    for g in range(GQA_GROUP):
        a = acc_ref[g * tq:(g + 1) * tq, :]
        outs.append(a / pltpu.roll(a, HEAD_DIM, 1))
    for j in range(GQA_GROUP // 2):
        pair = jnp.where(low, outs[2 * j], pltpu.roll(outs[2 * j + 1], HEAD_DIM, 1))
        o_ref[0, :, j * LANES:(j + 1) * LANES] = pair.astype(o_ref.dtype)


def _gqa_attention(act, tq=512, tk=1024):
    b, n, _ = act.shape
    width = GQA_GROUP * HEAD_DIM
    mrows = GQA_GROUP * tq
    return pl.pallas_call(
        functools.partial(_gqa_kernel, tq=tq, tk=tk),
        grid=(b, N_GQA_KV, n // tq),
        in_specs=[
            pl.BlockSpec((1, tq, width), lambda bi, h, i: (bi, i, h)),
            pl.BlockSpec((1, n, LANES), lambda bi, h, i: (bi, 0, KA_BLK + h)),
            pl.BlockSpec((1, n, LANES), lambda bi, h, i: (bi, 0, VA_BLK + h)),
        ],
        out_specs=pl.BlockSpec((1, tq, width), lambda bi, h, i: (bi, i, h)),
        out_shape=jax.ShapeDtypeStruct((b, n, GQA_Q_COLS), BF16),
        scratch_shapes=_flash_scratch(mrows, tk, LANES),
        compiler_params=pltpu.CompilerParams(
            dimension_semantics=("arbitrary", "arbitrary", "arbitrary"),
            vmem_limit_bytes=VMEM_LIMIT),
        name="gqa_attention",
    )(act, act, act)


def _rel_bucket(rel):
    half = NUM_BUCKETS // 2
    max_exact = half // 2
    n = jnp.minimum(jnp.abs(rel), MAX_DISTANCE)
    n2 = n * n
    large = jnp.full(rel.shape, max_exact, jnp.int32)
    for k in range(1, half - max_exact):
        large = large + (n2 >= (max_exact * max_exact) * (2 ** k)).astype(jnp.int32)
    return jnp.where(rel > 0, half, 0) + jnp.where(n < max_exact, n, large)


def _bias_geometry(tb, tk, n_sub):
    assert tk % tb == 0
    ratio = tk // tb
    e_lo = (-(MAX_DISTANCE - 1) - tk) // tb
    e_hi = -((-(MAX_DISTANCE - 1) - tb) // tb)
    spans = [-(-(j + n_sub - 1 + e_hi) // ratio) - 1 - (j + e_lo) // ratio
             for j in range(0, ratio * n_sub, n_sub)]
    return e_lo, e_hi, max(spans) + max(spans) % 2


def _diff_kernel(rb_ref, lam_ref, gs_ref, q_ref, k_ref, v_ref, o_ref,
                 qz_ref, m_ref, acc_ref, s0_ref, s1_ref, mc0_ref, mc1_ref, bias_ref, lsum_ref,
                 *, t, tk, lam_init):
    h = pl.program_id(0)
    bi = pl.program_id(1)
    i = pl.program_id(2)
    n = k_ref.shape[1]
    nc = n // tk
    tb = BLOCK_ROWS
    n_sub = t // tb
    ratio = tk // tb
    e_lo, e_hi, near_chunks = _bias_geometry(tb, tk, n_sub)
    shift_before = rb_ref[NUM_BUCKETS // 2 - 1, h] * LOG2E
    shift_after = rb_ref[NUM_BUCKETS - 1, h] * LOG2E

    @pl.when(jnp.logical_and(bi == 0, i == 0))
    def _():
        width = (e_hi - 1 - e_lo) * tb + tk
        bucket = _rel_bucket(e_lo * tb + lax.broadcasted_iota(jnp.int32, (8, width), 1))
        gen = jnp.zeros((8, width), F32)
        for bkt in range(NUM_BUCKETS):
            gen = jnp.where(bucket == bkt, rb_ref[bkt, h] * LOG2E, gen)
        sub = lax.broadcasted_iota(jnp.int32, (8, width), 0)
        rows8 = gen
        for r in range(1, 8):
            rows8 = jnp.where(sub == r, pltpu.roll(gen, r, 1), rows8)
        for r0 in range(0, tb, 8):
            blk = rows8 if r0 == 0 else pltpu.roll(rows8, r0, 1)
            for e in range(e_lo + 1, e_hi):
                off = (e - e_lo) * tb
                bias_ref[e - e_lo, r0:r0 + 8, :] = blk[:, off:off + tk]
        bias_ref[0] = jnp.full((tb, tk), shift_before, F32)
        bias_ref[e_hi - e_lo] = jnp.full((tb, tk), shift_after, F32)

    lane = lax.broadcasted_iota(jnp.int32, (t, LANES), 1)
    low = lane < HEAD_DIM
    q = q_ref[0]
    zero = jnp.zeros_like(q)
    qz_ref[0:t, :] = jnp.where(low, q, zero)
    qz_ref[t:2 * t, :] = jnp.where(low, zero, q)
    m_ref[...] = jnp.full(m_ref.shape, M_INIT, F32)
    acc_ref[...] = jnp.zeros(acc_ref.shape, F32)
    lsum_ref[...] = jnp.zeros(lsum_ref.shape, F32)

    n_near = min(near_chunks, nc)
    near0 = jnp.clip(jnp.floor_divide(i * n_sub + e_lo, ratio) + 1, 0, nc - n_near)

    def tail_bias(c, rows):
        block = i * n_sub + (rows.start % t) // tb
        return bias_ref[jnp.clip(ratio * c - block, e_lo, e_hi) - e_lo]

    _flash_sweep(
        nc, tk, qz_ref, k_ref, v_ref,
        tuple(slice(r, r + BLOCK_ROWS) for r in range(0, 2 * t, BLOCK_ROWS)),
        ((s0_ref, mc0_ref), (s1_ref, mc1_ref)), m_ref, (acc_ref, lsum_ref),
        v_ones=jnp.ones((tk, LANES), BF16), n_tail=n_near,
        main_chunk=lambda pos: pos + jnp.where(pos >= near0, n_near, 0),
        main_shift=lambda c: jnp.where(c < near0, shift_before, shift_after),
        tail_chunk=lambda u: near0 + u,
        tail_bias=tail_bias)

    lv = lam_ref[...]
    lam = (jnp.exp(jnp.sum(lv[0:1, :] * lv[1:2, :], axis=-1, keepdims=True))
           - jnp.exp(jnp.sum(lv[2:3, :] * lv[3:4, :], axis=-1, keepdims=True)) + lam_init)
    o1 = acc_ref[0:t, :] / lsum_ref[0:t, :]
    o2 = acc_ref[t:2 * t, :] / lsum_ref[t:2 * t, :]
    o = o1 - lam * o2
    o_ref[0] = (_rms(o, gs_ref[...]) * (1.0 - lam_init)).astype(o_ref.dtype)


def _diff_attention(act, rel_bias, lam_rows, g_subln, lam_init, t=1024, tk=1024):
    b, n, _ = act.shape
    e_lo, e_hi, _ = _bias_geometry(BLOCK_ROWS, tk, t // BLOCK_ROWS)
    return pl.pallas_call(
        functools.partial(_diff_kernel, t=t, tk=tk, lam_init=lam_init),
        grid=(N_DIFF_HEADS, b, n // t),
        in_specs=[
            pl.BlockSpec(memory_space=pltpu.SMEM),
            pl.BlockSpec((MOD_ROWS, LANES), lambda h, bi, i: (0, 0)),
            pl.BlockSpec((1, LANES), lambda h, bi, i: (0, 0)),
            pl.BlockSpec((1, t, LANES), lambda h, bi, i: (bi, i, QD_BLK + h)),
            pl.BlockSpec((1, n, LANES), lambda h, bi, i: (bi, 0, KD_BLK + h),
                         pipeline_mode=pl.Buffered(1)),
            pl.BlockSpec((1, n, LANES), lambda h, bi, i: (bi, 0, VD_BLK + h),
                         pipeline_mode=pl.Buffered(1)),
        ],
        out_specs=pl.BlockSpec((1, t, LANES), lambda h, bi, i: (bi, i, h)),
        out_shape=jax.ShapeDtypeStruct((b, n, DIFF_COLS), BF16),
        scratch_shapes=_flash_scratch(2 * t, tk, LANES)
        + [pltpu.VMEM((e_hi - e_lo + 1, BLOCK_ROWS, tk), F32),
           pltpu.VMEM((2 * t, LANES), F32)],
        compiler_params=pltpu.CompilerParams(
            dimension_semantics=("arbitrary", "arbitrary", "arbitrary"),
            vmem_limit_bytes=VMEM_LIMIT),
        name="diff_attention",
    )(rel_bias, lam_rows, g_subln, act, act, act)


def _ffn_kernel(x_ref, oa_ref, od_ref, mod_ref, wout_ref, gpm_ref, gpf_ref, wgu_ref, wdn_ref,
                gpo_ref, y_ref):
    gt1 = mod_ref[0, 2:3, :]
    sh2 = mod_ref[0, 3:4, :]
    sc2 = mod_ref[0, 4:5, :]
    gt2 = mod_ref[0, 5:6, :]
    wa = oa_ref.shape[2]
    d_ff = wdn_ref.shape[0]
    tm = x_ref.shape[1]
    groups = [slice(r0, r0 + FFN_GROUP_ROWS) for r0 in range(0, tm, FFN_GROUP_ROWS)]
    mixes = [jnp.dot(oa_ref[0, rows, :], wout_ref[0:wa, :], preferred_element_type=F32)
             + jnp.dot(od_ref[0, rows, :], wout_ref[wa:, :], preferred_element_type=F32)
             for rows in groups]
    x1s, gus, fs = [], [], []
    for rows, mix in zip(groups, mixes):
        x1 = x_ref[0, rows, :] + gt1 * _rms(mix, gpm_ref[...])
        h = _rms(x1, gpf_ref[...]) * (1.0 + sc2) + sh2
        x1s.append(x1)
        gus.append(jnp.dot(h.astype(BF16), wgu_ref[...], preferred_element_type=F32))
    for gu in gus:
        gate = gu[:, :d_ff]
        up = gu[:, d_ff:]
        act = gate * (1.0 / (1.0 + jnp.exp(-gate))) * up
        fs.append(jnp.dot(act.astype(BF16), wdn_ref[...], preferred_element_type=F32))
    for rows, x1, f in zip(groups, x1s, fs):
        y_ref[0, rows, :] = x1 + gt2 * _rms(f, gpo_ref[...])


def _out_ffn(x, out_a, out_d, mod3, w_out_bf, g_post_mix, g_pre_ffn, w_gu_bf, w_down_bf,
             g_post_ffn, tm=512):
    b, n, d = x.shape
    const = lambda bi, i: (0, 0)
    once = pl.Buffered(1)
    return pl.pallas_call(
        _ffn_kernel,
        grid=(b, n // tm),
        in_specs=[
            pl.BlockSpec((1, tm, d), lambda bi, i: (bi, i, 0)),
            pl.BlockSpec((1, tm, out_a.shape[2]), lambda bi, i: (bi, i, 0)),
            pl.BlockSpec((1, tm, out_d.shape[2]), lambda bi, i: (bi, i, 0)),
            pl.BlockSpec((1, 6, d), lambda bi, i: (bi, 0, 0)),
            pl.BlockSpec(w_out_bf.shape, const, pipeline_mode=once),
            pl.BlockSpec((1, d), const),
            pl.BlockSpec((1, d), const),
            pl.BlockSpec(w_gu_bf.shape, const, pipeline_mode=once),
            pl.BlockSpec(w_down_bf.shape, const, pipeline_mode=once),
            pl.BlockSpec((1, d), const),
        ],
        out_specs=pl.BlockSpec((1, tm, d), lambda bi, i: (bi, i, 0)),
        out_shape=jax.ShapeDtypeStruct((b, n, d), F32),
        compiler_params=pltpu.CompilerParams(
            dimension_semantics=("arbitrary", "arbitrary"), vmem_limit_bytes=VMEM_LIMIT),
        name="out_ffn",
    )(x, out_a, out_d, mod3, w_out_bf, g_post_mix, g_pre_ffn, w_gu_bf, w_down_bf, g_post_ffn)


def _rope_tables(n):
    rows = n // GRID_W
    half = HEAD_DIM // 2
    quarter = half // 2
    inv = ROPE_THETA ** (-jnp.arange(0, half, 2, dtype=F32) / half)
    lane = jnp.arange(LANES)
    inv_lane = inv[lane % quarter][None, :]
    ang_row = jnp.arange(rows, dtype=F32)[:, None] * inv_lane
    ang_col = jnp.arange(GRID_W, dtype=F32)[:, None] * inv_lane
    is_row = (lane % HEAD_DIM < half)[None, None, :]
    sign = jnp.where(lane % half < quarter, -1.0, 1.0).astype(F32)

    def table(fn, scale):
        by_row = (fn(ang_row) * scale)[:, None, :]
        by_col = (fn(ang_col) * scale)[None, :, :]
        return jnp.where(is_row, by_row, by_col).reshape(n, LANES)

    return table(jnp.cos, 1.0), table(jnp.sin, sign[None, :])


def kernel(x_prompt, x_sample, c_prompt, c_sample, rel_bias, w_ada, b_ada, g_pre_mix, w_in,
           g_q, g_k, lam_q1, lam_k1, lam_q2, lam_k2, g_subln, w_out, g_post_mix, g_pre_ffn,
           w_gu, w_down, g_post_ffn):
    depth = w_ada.shape[0]
    d = x_prompt.shape[-1]
    xs = [x_prompt, x_sample]
    cs = [c_prompt, c_sample]
    n_c = sum(c.shape[0] for c in cs)
    assert n_c <= MOD_ROWS
    c_all = jnp.concatenate(cs + [jnp.zeros((MOD_ROWS - n_c, d), F32)], axis=0)
    seg = (jnp.arange(LANES)[:, None] // HEAD_DIM
           == jnp.arange(LANES)[None, :] // HEAD_DIM).astype(BF16)
    tables = [_rope_tables(max(x.shape[1] for x in xs))] * len(xs)

    for l in range(depth):
        lam_init = 0.8 - 0.6 * math.exp(-0.3 * l)
        mod = _modulation(c_all, w_ada, b_ada[l], l).reshape(MOD_ROWS, 6, d)
        w_in_bf = w_in[l].astype(BF16)
        w_out_bf = w_out[l].astype(BF16)
        w_gu_bf = w_gu[l].astype(BF16)
        w_down_bf = w_down[l].astype(BF16)
        gq2 = jnp.tile(g_q[l], LANES // HEAD_DIM).reshape(1, LANES)
        gk2 = jnp.tile(g_k[l], LANES // HEAD_DIM).reshape(1, LANES)
        lam_rows = jnp.zeros((MOD_ROWS, LANES), F32).at[0:4, 0:HEAD_DIM].set(
            jnp.stack([lam_q1[l], lam_k1[l], lam_q2[l], lam_k2[l]]))
        new_xs = []
        row0 = 0
        for x, (cos2, sin2) in zip(xs, tables):
            mod3 = mod[row0:row0 + x.shape[0]]
            row0 += x.shape[0]
            act = _in_projection(x, mod3, g_pre_mix[l].reshape(1, d), w_in_bf, gq2, gk2,
                                 cos2, sin2, seg)
            out_a = _gqa_attention(act)
            out_d = _diff_attention(act, rel_bias, lam_rows, g_subln[l].reshape(1, LANES),
                                    lam_init)
            new_xs.append(_out_ffn(x, out_a, out_d, mod3, w_out_bf,
                                   g_post_mix[l].reshape(1, d), g_pre_ffn[l].reshape(1, d),
                                   w_gu_bf, w_down_bf, g_post_ffn[l].reshape(1, d)))
        xs = new_xs
    return tuple(xs)
```

```python
import functools
import math

import jax
import jax.numpy as jnp
from jax import lax
from jax.experimental import pallas as pl
from jax.experimental.pallas import tpu as pltpu

F32 = jnp.float32
BF16 = jnp.bfloat16

HEAD_DIM = 64
N_GQA_HEADS = 8
N_GQA_KV = 2
GQA_GROUP = N_GQA_HEADS // N_GQA_KV
N_DIFF_HEADS = 4
GRID_W = 64
NUM_BUCKETS = 32
MAX_DISTANCE = 128
ROPE_THETA = 10000.0
EPS = 1e-6
ATTN_SCALE = 1.0 / math.sqrt(HEAD_DIM)
LOG2E = 1.4426950408889634
Q_SCALE = ATTN_SCALE * LOG2E

LANES = 128
GQA_Q_COLS = N_GQA_HEADS * HEAD_DIM
GQA_KV_COLS = N_GQA_KV * HEAD_DIM
DIFF_COLS = N_DIFF_HEADS * 2 * HEAD_DIM

QA_BLK = 0
KA_BLK = QA_BLK + GQA_Q_COLS // LANES
VA_BLK = KA_BLK + N_GQA_KV
QD_BLK = VA_BLK + N_GQA_KV
KD_BLK = QD_BLK + N_DIFF_HEADS
VD_BLK = KD_BLK + N_DIFF_HEADS
ACT_COLS = (VD_BLK + N_DIFF_HEADS) * LANES

MOD_ROWS = 8
M_INIT = -0.5 * float(jnp.finfo(jnp.float32).max)

VMEM_LIMIT = 56 * 1024 * 1024
FLASH_UNROLL = 2
STRIP_ROWS = 32
BLOCK_ROWS = 512
FFN_GROUP_ROWS = 256


def _rms(x, g):
    return x * lax.rsqrt(jnp.mean(x * x, axis=-1, keepdims=True) + EPS) * g


def _mod_kernel(c_ref, w_ref, b_ref, o_ref):
    c = c_ref[...]
    a = c * (1.0 / (1.0 + jnp.exp(-c)))
    a_hi = a.astype(BF16)
    a_lo = (a - a_hi.astype(F32)).astype(BF16)
    w = w_ref[0]
    w_hi = w.astype(BF16)
    w_lo = (w - w_hi.astype(F32)).astype(BF16)
    acc = jnp.dot(a_hi, w_hi, preferred_element_type=F32)
    acc += jnp.dot(a_lo, w_hi, preferred_element_type=F32)
    acc += jnp.dot(a_hi, w_lo, preferred_element_type=F32)
    o_ref[...] = acc + b_ref[...]


def _modulation(c_all, w_ada, b_ada, layer):
    rows, d = c_all.shape
    n_out = w_ada.shape[2]
    tn = 1536
    return pl.pallas_call(
        _mod_kernel,
        grid=(n_out // tn,),
        in_specs=[
            pl.BlockSpec((rows, d), lambda j: (0, 0)),
            pl.BlockSpec((1, d, tn), lambda j: (layer, 0, j)),
            pl.BlockSpec((1, tn), lambda j: (0, j)),
        ],
        out_specs=pl.BlockSpec((rows, tn), lambda j: (0, j)),
        out_shape=jax.ShapeDtypeStruct((rows, n_out), F32),
        compiler_params=pltpu.CompilerParams(
            dimension_semantics=("arbitrary",), vmem_limit_bytes=VMEM_LIMIT),
        name="modulation",
    )(c_all, w_ada, b_ada.reshape(1, n_out))


def _inproj_kernel(x_ref, mod_ref, g_ref, w_ref, gq_ref, gk_ref, cos_ref, sin_ref, seg_ref,
                   o_ref):
    for r0 in range(0, x_ref.shape[1], FFN_GROUP_ROWS):
        _inproj_rows(slice(r0, r0 + FFN_GROUP_ROWS), x_ref, mod_ref, g_ref, w_ref, gq_ref,
                     gk_ref, cos_ref, sin_ref, seg_ref, o_ref)


def _inproj_rows(rows, x_ref, mod_ref, g_ref, w_ref, gq_ref, gk_ref, cos_ref, sin_ref, seg_ref,
                 o_ref):
    x = x_ref[0, rows, :]
    sh = mod_ref[0, 0:1, :]
    sc = mod_ref[0, 1:2, :]
    h = _rms(x, g_ref[...]) * (1.0 + sc) + sh
    proj = jnp.dot(h.astype(BF16), w_ref[...], preferred_element_type=F32)

    cos = cos_ref[rows, :]
    sin = sin_ref[rows, :]
    seg = seg_ref[...]
    lane = lax.broadcasted_iota(jnp.int32, (x.shape[0], LANES), 1)
    first_half = (lane & 16) == 0
    low = lane < HEAD_DIM

    def norm_rope(t, g):
        ss = jnp.dot((t * t).astype(BF16), seg, preferred_element_type=F32)
        tn = t * lax.rsqrt(ss * (1.0 / HEAD_DIM) + EPS) * g
        rot = jnp.where(first_half, pltpu.roll(tn, LANES - 16, 1), pltpu.roll(tn, 16, 1))
        return tn * cos + rot * sin

    def put(blk, val):
        o_ref[0, rows, blk * LANES:(blk + 1) * LANES] = val.astype(BF16)

    col = 0
    for j in range(GQA_Q_COLS // LANES):
        put(QA_BLK + j, norm_rope(proj[:, col:col + LANES], gq_ref[...]) * Q_SCALE)
        col += LANES
    kr = norm_rope(proj[:, col:col + LANES], gk_ref[...])
    col += LANES
    kr_sw = pltpu.roll(kr, HEAD_DIM, 1)
    put(KA_BLK + 0, jnp.where(low, kr, kr_sw))
    put(KA_BLK + 1, jnp.where(low, kr_sw, kr))
    va = proj[:, col:col + LANES]
    col += LANES
    va_sw = pltpu.roll(va, HEAD_DIM, 1)
    put(VA_BLK + 0, jnp.where(low, va, 1.0))
    put(VA_BLK + 1, jnp.where(low, va_sw, 1.0))
    for j in range(N_DIFF_HEADS):
        put(QD_BLK + j, proj[:, col:col + LANES] * Q_SCALE)
        col += LANES
    for j in range(N_DIFF_HEADS):
        put(KD_BLK + j, proj[:, col:col + LANES])
        col += LANES
    for j in range(N_DIFF_HEADS):
        put(VD_BLK + j, proj[:, col:col + LANES])
        col += LANES


def _in_projection(x, mod3, g_pre, w_in_bf, gq2, gk2, cos2, sin2, seg, tm=512):
    b, n, d = x.shape
    in_w = w_in_bf.shape[1]
    const = lambda bi, i: (0, 0)
    return pl.pallas_call(
        _inproj_kernel,
        grid=(b, n // tm),
        in_specs=[
            pl.BlockSpec((1, tm, d), lambda bi, i: (bi, i, 0)),
            pl.BlockSpec((1, 6, d), lambda bi, i: (bi, 0, 0)),
            pl.BlockSpec((1, d), const),
            pl.BlockSpec((d, in_w), const),
            pl.BlockSpec((1, LANES), const),
            pl.BlockSpec((1, LANES), const),
            pl.BlockSpec((tm, LANES), lambda bi, i: (i, 0)),
            pl.BlockSpec((tm, LANES), lambda bi, i: (i, 0)),
            pl.BlockSpec((LANES, LANES), const),
        ],
        out_specs=pl.BlockSpec((1, tm, ACT_COLS), lambda bi, i: (bi, i, 0)),
        out_shape=jax.ShapeDtypeStruct((b, n, ACT_COLS), BF16),
        compiler_params=pltpu.CompilerParams(
            dimension_semantics=("arbitrary", "arbitrary"), vmem_limit_bytes=VMEM_LIMIT),
        name="in_projection",
    )(x, mod3, g_pre, w_in_bf, gq2, gk2, cos2, sin2, seg)


def _flash_sweep(nc, tk, qz_ref, k_ref, v_ref, blocks, bufs, m_ref, acc_ref, *, v_ones=None,
                 n_tail=0, main_chunk=None, main_shift=None, tail_chunk=None, tail_bias=None):
    n_main = nc - n_tail
    unroll = min(FLASH_UNROLL, n_main) if n_main else 0
    assert n_tail % 2 == 0 and (n_main == 0 or (unroll % 2 == 0 and n_main % unroll == 0))

    def chunk_of(kind, idx):
        if kind == "tail":
            return tail_chunk(idx), None, tail_bias
        c = idx if main_chunk is None else main_chunk(idx)
        return c, (None if main_shift is None else main_shift(c)), None

    def scores(kind, idx, s_ref, mc_ref):
        c, shift, bias = chunk_of(kind, idx)
        start = pl.multiple_of(c * tk, tk)
        kc = k_ref[0, pl.ds(start, tk), :]
        for rows in blocks:
            s = lax.dot_general(qz_ref[rows, :], kc, (((1,), (1,)), ((), ())),
                                preferred_element_type=F32)
            if bias is not None:
                s = s + bias(c, rows)
            s_ref[rows, :] = s
            mc = jnp.max(s, axis=-1, keepdims=True)
            if shift is not None:
                mc = mc + shift
            mc_ref[rows, :] = jnp.broadcast_to(mc, (s.shape[0], LANES))

    def values(kind, idx, s_ref, mc_ref):
        c, shift, _ = chunk_of(kind, idx)
        start = pl.multiple_of(c * tk, tk)
        vc = v_ref[0, pl.ds(start, tk), :]
        if v_ones is not None:
            vc = jnp.concatenate([vc, v_ones], axis=1)
        reps = acc_ref.shape[1] // LANES
        for rows in blocks:
            strips = []
            for r0 in range(rows.start, rows.stop, STRIP_ROWS):
                rs = slice(r0, r0 + STRIP_ROWS)
                m_prev = m_ref[rs, :]
                m_new = jnp.maximum(m_prev, mc_ref[rs, :])
                alpha = jnp.exp2(m_prev - m_new)
                m_ref[rs, :] = m_new
                for j in range(reps):
                    cols = slice(j * LANES, (j + 1) * LANES)
                    acc_ref[rs, cols] = alpha * acc_ref[rs, cols]
                m_sub = m_new if shift is None else m_new - shift
                strips.append(jnp.concatenate(
                    [jnp.exp2(s_ref[rs, j * LANES:(j + 1) * LANES] - m_sub).astype(BF16)
                     for j in range(tk // LANES)], axis=1))
            p = jnp.concatenate(strips, axis=0)
            acc_ref[rows, :] += jnp.dot(p, vc, preferred_element_type=F32)

    def group(kind, base, count, nxt):
        for u in range(count):
            if u + 1 < count:
                scores(kind, base + u + 1, *bufs[(u + 1) % 2])
            elif nxt is not None:
                scores(*nxt, *bufs[(u + 1) % 2])
            values(kind, base + u, *bufs[u % 2])

    tail_start = ("tail", 0) if n_tail else None
    scores(*(("main", 0) if n_main else tail_start), *bufs[0])
    if n_main:
        n_groups = n_main // unroll

        def body(g, carry):
            group("main", g * unroll, unroll, ("main", (g + 1) * unroll))
            return carry

        lax.fori_loop(0, n_groups - 1, body, 0)
        group("main", n_main - unroll, unroll, tail_start)
    if n_tail:
        group("tail", 0, n_tail, None)


def _flash_scratch(mrows, tk, acc_cols):
    return [
        pltpu.VMEM((mrows, LANES), BF16),
        pltpu.VMEM((mrows, LANES), F32),
        pltpu.VMEM((mrows, acc_cols), F32),
        pltpu.VMEM((mrows, tk), F32),
        pltpu.VMEM((mrows, tk), F32),
        pltpu.VMEM((mrows, LANES), F32),
        pltpu.VMEM((mrows, LANES), F32),
    ]


def _gqa_kernel(q_ref, k_ref, v_ref, o_ref, qz_ref, m_ref, acc_ref, s0_ref, s1_ref,
                mc0_ref, mc1_ref, *, tq, tk):
    n = k_ref.shape[1]
    mrows = GQA_GROUP * tq
    lane = lax.broadcasted_iota(jnp.int32, (tq, LANES), 1)
    low = lane < HEAD_DIM
    for g in range(GQA_GROUP):
        slab = q_ref[0, :, (g // 2) * LANES:(g // 2 + 1) * LANES]
        keep = low if g % 2 == 0 else jnp.logical_not(low)
        qz_ref[g * tq:(g + 1) * tq, :] = jnp.where(keep, slab, jnp.zeros_like(slab))
    m_ref[...] = jnp.full(m_ref.shape, M_INIT, F32)
    acc_ref[...] = jnp.zeros(acc_ref.shape, F32)

    blocks = tuple(slice(r, r + BLOCK_ROWS) for r in range(0, mrows, BLOCK_ROWS))
    _flash_sweep(n // tk, tk, qz_ref, k_ref, v_ref, blocks,
                 ((s0_ref, mc0_ref), (s1_ref, mc1_ref)), m_ref, acc_ref)

    outs = []
    for g in range(GQA_GROUP):
        a = acc_ref[g * tq:(g + 1) * tq, :]
        outs.append(a / pltpu.roll(a, HEAD_DIM, 1))
    for j in range(GQA_GROUP // 2):
        pair = jnp.where(low, outs[2 * j], pltpu.roll(outs[2 * j + 1], HEAD_DIM, 1))
        o_ref[0, :, j * LANES:(j + 1) * LANES] = pair.astype(o_ref.dtype)


def _gqa_attention(act, tq=512, tk=1024):
    b, n, _ = act.shape
    width = GQA_GROUP * HEAD_DIM
    mrows = GQA_GROUP * tq
    return pl.pallas_call(
        functools.partial(_gqa_kernel, tq=tq, tk=tk),
        grid=(b, N_GQA_KV, n // tq),
        in_specs=[
            pl.BlockSpec((1, tq, width), lambda bi, h, i: (bi, i, h)),
            pl.BlockSpec((1, n, LANES), lambda bi, h, i: (bi, 0, KA_BLK + h)),
            pl.BlockSpec((1, n, LANES), lambda bi, h, i: (bi, 0, VA_BLK + h)),
        ],
        out_specs=pl.BlockSpec((1, tq, width), lambda bi, h, i: (bi, i, h)),
        out_shape=jax.ShapeDtypeStruct((b, n, GQA_Q_COLS), BF16),
        scratch_shapes=_flash_scratch(mrows, tk, LANES),
        compiler_params=pltpu.CompilerParams(
            dimension_semantics=("arbitrary", "arbitrary", "arbitrary"),
            vmem_limit_bytes=VMEM_LIMIT),
        name="gqa_attention",
    )(act, act, act)


def _rel_bucket(rel):
    half = NUM_BUCKETS // 2
    max_exact = half // 2
    n = jnp.minimum(jnp.abs(rel), MAX_DISTANCE)
    n2 = n * n
    large = jnp.full(rel.shape, max_exact, jnp.int32)
    for k in range(1, half - max_exact):
        large = large + (n2 >= (max_exact * max_exact) * (2 ** k)).astype(jnp.int32)
    return jnp.where(rel > 0, half, 0) + jnp.where(n < max_exact, n, large)


def _bias_geometry(tb, tk, n_sub):
    assert tk % tb == 0
    ratio = tk // tb
    e_lo = (-(MAX_DISTANCE - 1) - tk) // tb
    e_hi = -((-(MAX_DISTANCE - 1) - tb) // tb)
    spans = [-(-(j + n_sub - 1 + e_hi) // ratio) - 1 - (j + e_lo) // ratio
             for j in range(0, ratio * n_sub, n_sub)]
    return e_lo, e_hi, max(spans) + max(spans) % 2


def _diff_kernel(rb_ref, lam_ref, gs_ref, q_ref, k_ref, v_ref, o_ref,
                 qz_ref, m_ref, acc_ref, s0_ref, s1_ref, mc0_ref, mc1_ref, bias_ref,
                 *, t, tk, lam_init):
    h = pl.program_id(0)
    bi = pl.program_id(1)
    i = pl.program_id(2)
    n = k_ref.shape[1]
    nc = n // tk
    tb = BLOCK_ROWS
    n_sub = t // tb
    ratio = tk // tb
    e_lo, e_hi, near_chunks = _bias_geometry(tb, tk, n_sub)
    shift_before = rb_ref[NUM_BUCKETS // 2 - 1, h] * LOG2E
    shift_after = rb_ref[NUM_BUCKETS - 1, h] * LOG2E

    @pl.when(jnp.logical_and(bi == 0, i == 0))
    def _():
        width = (e_hi - 1 - e_lo) * tb + tk
        bucket = _rel_bucket(e_lo * tb + lax.broadcasted_iota(jnp.int32, (8, width), 1))
        gen = jnp.zeros((8, width), F32)
        for bkt in range(NUM_BUCKETS):
            gen = jnp.where(bucket == bkt, rb_ref[bkt, h] * LOG2E, gen)
        sub = lax.broadcasted_iota(jnp.int32, (8, width), 0)
        rows8 = gen
        for r in range(1, 8):
            rows8 = jnp.where(sub == r, pltpu.roll(gen, r, 1), rows8)
        for r0 in range(0, tb, 8):
            blk = rows8 if r0 == 0 else pltpu.roll(rows8, r0, 1)
            for e in range(e_lo + 1, e_hi):
                off = (e - e_lo) * tb
                bias_ref[e - e_lo, r0:r0 + 8, :] = blk[:, off:off + tk]
        bias_ref[0] = jnp.full((tb, tk), shift_before, F32)
        bias_ref[e_hi - e_lo] = jnp.full((tb, tk), shift_after, F32)

    lane = lax.broadcasted_iota(jnp.int32, (t, LANES), 1)
    low = lane < HEAD_DIM
    q = q_ref[0]
    zero = jnp.zeros_like(q)
    qz_ref[0:t, :] = jnp.where(low, q, zero)
    qz_ref[t:2 * t, :] = jnp.where(low, zero, q)
    m_ref[...] = jnp.full(m_ref.shape, M_INIT, F32)
    acc_ref[...] = jnp.zeros(acc_ref.shape, F32)

    n_near = min(near_chunks, nc)
    near0 = jnp.clip(jnp.floor_divide(i * n_sub + e_lo, ratio) + 1, 0, nc - n_near)

    def tail_bias(c, rows):
        block = i * n_sub + (rows.start % t) // tb
        return bias_ref[jnp.clip(ratio * c - block, e_lo, e_hi) - e_lo]

    _flash_sweep(
        nc, tk, qz_ref, k_ref, v_ref,
        tuple(slice(r, r + BLOCK_ROWS) for r in range(0, 2 * t, BLOCK_ROWS)),
        ((s0_ref, mc0_ref), (s1_ref, mc1_ref)), m_ref, acc_ref,
        v_ones=jnp.ones((tk, LANES), BF16), n_tail=n_near,
        main_chunk=lambda pos: pos + jnp.where(pos >= near0, n_near, 0),
        main_shift=lambda c: jnp.where(c < near0, shift_before, shift_after),
        tail_chunk=lambda u: near0 + u,
        tail_bias=tail_bias)

    lv = lam_ref[...]
    lam = (jnp.exp(jnp.sum(lv[0:1, :] * lv[1:2, :], axis=-1, keepdims=True))
           - jnp.exp(jnp.sum(lv[2:3, :] * lv[3:4, :], axis=-1, keepdims=True)) + lam_init)
    o1 = acc_ref[0:t, 0:LANES] / acc_ref[0:t, LANES:2 * LANES]
    o2 = acc_ref[t:2 * t, 0:LANES] / acc_ref[t:2 * t, LANES:2 * LANES]
    o = o1 - lam * o2
    o_ref[0] = (_rms(o, gs_ref[...]) * (1.0 - lam_init)).astype(o_ref.dtype)


def _diff_attention(act, rel_bias, lam_rows, g_subln, lam_init, t=1024, tk=1024):
    b, n, _ = act.shape
    e_lo, e_hi, _ = _bias_geometry(BLOCK_ROWS, tk, t // BLOCK_ROWS)
    return pl.pallas_call(
        functools.partial(_diff_kernel, t=t, tk=tk, lam_init=lam_init),
        grid=(N_DIFF_HEADS, b, n // t),
        in_specs=[
            pl.BlockSpec(memory_space=pltpu.SMEM),
            pl.BlockSpec((MOD_ROWS, LANES), lambda h, bi, i: (0, 0)),
            pl.BlockSpec((1, LANES), lambda h, bi, i: (0, 0)),
            pl.BlockSpec((1, t, LANES), lambda h, bi, i: (bi, i, QD_BLK + h)),
            pl.BlockSpec((1, n, LANES), lambda h, bi, i: (bi, 0, KD_BLK + h),
                         pipeline_mode=pl.Buffered(1)),
            pl.BlockSpec((1, n, LANES), lambda h, bi, i: (bi, 0, VD_BLK + h),
                         pipeline_mode=pl.Buffered(1)),
        ],
        out_specs=pl.BlockSpec((1, t, LANES), lambda h, bi, i: (bi, i, h)),
        out_shape=jax.ShapeDtypeStruct((b, n, DIFF_COLS), BF16),
        scratch_shapes=_flash_scratch(2 * t, tk, 2 * LANES)
        + [pltpu.VMEM((e_hi - e_lo + 1, BLOCK_ROWS, tk), F32)],
        compiler_params=pltpu.CompilerParams(
            dimension_semantics=("arbitrary", "arbitrary", "arbitrary"),
            vmem_limit_bytes=VMEM_LIMIT),
        name="diff_attention",
    )(rel_bias, lam_rows, g_subln, act, act, act)


def _ffn_kernel(x_ref, oa_ref, od_ref, mod_ref, wout_ref, gpm_ref, gpf_ref, wgu_ref, wdn_ref,
                gpo_ref, y_ref):
    gt1 = mod_ref[0, 2:3, :]
    sh2 = mod_ref[0, 3:4, :]
    sc2 = mod_ref[0, 4:5, :]
    gt2 = mod_ref[0, 5:6, :]
    wa = oa_ref.shape[2]
    d_ff = wdn_ref.shape[0]
    tm = x_ref.shape[1]
    groups = [slice(r0, r0 + FFN_GROUP_ROWS) for r0 in range(0, tm, FFN_GROUP_ROWS)]
    mixes = [jnp.dot(oa_ref[0, rows, :], wout_ref[0:wa, :], preferred_element_type=F32)
             + jnp.dot(od_ref[0, rows, :], wout_ref[wa:, :], preferred_element_type=F32)
             for rows in groups]
    x1s, gus, fs = [], [], []
    for rows, mix in zip(groups, mixes):
        x1 = x_ref[0, rows, :] + gt1 * _rms(mix, gpm_ref[...])
        h = _rms(x1, gpf_ref[...]) * (1.0 + sc2) + sh2
        x1s.append(x1)
        gus.append(jnp.dot(h.astype(BF16), wgu_ref[...], preferred_element_type=F32))
    for gu in gus:
        gate = gu[:, :d_ff]
        up = gu[:, d_ff:]
        act = gate * (1.0 / (1.0 + jnp.exp(-gate))) * up
        fs.append(jnp.dot(act.astype(BF16), wdn_ref[...], preferred_element_type=F32))
    for rows, x1, f in zip(groups, x1s, fs):
        y_ref[0, rows, :] = x1 + gt2 * _rms(f, gpo_ref[...])


def _out_ffn(x, out_a, out_d, mod3, w_out_bf, g_post_mix, g_pre_ffn, w_gu_bf, w_down_bf,
             g_post_ffn, tm=512):
    b, n, d = x.shape
    const = lambda bi, i: (0, 0)
    once = pl.Buffered(1)
    return pl.pallas_call(
        _ffn_kernel,
        grid=(b, n // tm),
        in_specs=[
            pl.BlockSpec((1, tm, d), lambda bi, i: (bi, i, 0)),
            pl.BlockSpec((1, tm, out_a.shape[2]), lambda bi, i: (bi, i, 0)),
            pl.BlockSpec((1, tm, out_d.shape[2]), lambda bi, i: (bi, i, 0)),
            pl.BlockSpec((1, 6, d), lambda bi, i: (bi, 0, 0)),
            pl.BlockSpec(w_out_bf.shape, const, pipeline_mode=once),
            pl.BlockSpec((1, d), const),
            pl.BlockSpec((1, d), const),
            pl.BlockSpec(w_gu_bf.shape, const, pipeline_mode=once),
            pl.BlockSpec(w_down_bf.shape, const, pipeline_mode=once),
            pl.BlockSpec((1, d), const),
        ],
        out_specs=pl.BlockSpec((1, tm, d), lambda bi, i: (bi, i, 0)),
        out_shape=jax.ShapeDtypeStruct((b, n, d), F32),
        compiler_params=pltpu.CompilerParams(
            dimension_semantics=("arbitrary", "arbitrary"), vmem_limit_bytes=VMEM_LIMIT),
        name="out_ffn",
    )(x, out_a, out_d, mod3, w_out_bf, g_post_mix, g_pre_ffn, w_gu_bf, w_down_bf, g_post_ffn)


def _rope_tables(n):
    rows = n // GRID_W
    half = HEAD_DIM // 2
    quarter = half // 2
    lane = jnp.arange(LANES)
    inv_lane = (ROPE_THETA ** (-(2 * (lane % quarter)).astype(F32) / half))[None, :]
    ang_row = jnp.arange(rows, dtype=F32)[:, None] * inv_lane
    ang_col = jnp.arange(GRID_W, dtype=F32)[:, None] * inv_lane
    is_row = (lane % HEAD_DIM < half)[None, None, :]
    sign = jnp.where(lane % half < quarter, -1.0, 1.0).astype(F32)

    def table(fn, scale):
        by_row = (fn(ang_row) * scale)[:, None, :]
        by_col = (fn(ang_col) * scale)[None, :, :]
        return jnp.where(is_row, by_row, by_col).reshape(n, LANES)

    return table(jnp.cos, 1.0), table(jnp.sin, sign[None, :])


def kernel(x_prompt, x_sample, c_prompt, c_sample, rel_bias, w_ada, b_ada, g_pre_mix, w_in,
           g_q, g_k, lam_q1, lam_k1, lam_q2, lam_k2, g_subln, w_out, g_post_mix, g_pre_ffn,
           w_gu, w_down, g_post_ffn):
    depth = w_ada.shape[0]
    d = x_prompt.shape[-1]
    xs = [x_prompt, x_sample]
    cs = [c_prompt, c_sample]
    n_c = sum(c.shape[0] for c in cs)
    assert n_c <= MOD_ROWS
    c_all = jnp.concatenate(cs + [jnp.zeros((MOD_ROWS - n_c, d), F32)], axis=0)
    seg = (jnp.arange(LANES)[:, None] // HEAD_DIM
           == jnp.arange(LANES)[None, :] // HEAD_DIM).astype(BF16)
    tables = [_rope_tables(max(x.shape[1] for x in xs))] * len(xs)

    for l in range(depth):
        lam_init = 0.8 - 0.6 * math.exp(-0.3 * l)
        mod = _modulation(c_all, w_ada, b_ada[l], l).reshape(MOD_ROWS, 6, d)
        w_in_bf = w_in[l].astype(BF16)
        w_out_bf = w_out[l].astype(BF16)
        w_gu_bf = w_gu[l].astype(BF16)
        w_down_bf = w_down[l].astype(BF16)
        gq2 = jnp.tile(g_q[l], LANES // HEAD_DIM).reshape(1, LANES)
        gk2 = jnp.tile(g_k[l], LANES // HEAD_DIM).reshape(1, LANES)
        lam_rows = jnp.zeros((MOD_ROWS, LANES), F32).at[0:4, 0:HEAD_DIM].set(
            jnp.stack([lam_q1[l], lam_k1[l], lam_q2[l], lam_k2[l]]))
        new_xs = []
        row0 = 0
        for x, (cos2, sin2) in zip(xs, tables):
            mod3 = mod[row0:row0 + x.shape[0]]
            row0 += x.shape[0]
            act = _in_projection(x, mod3, g_pre_mix[l].reshape(1, d), w_in_bf, gq2, gk2,
                                 cos2, sin2, seg)
            out_a = _gqa_attention(act)
            out_d = _diff_attention(act, rel_bias, lam_rows, g_subln[l].reshape(1, LANES),
                                    lam_init)
            new_xs.append(_out_ffn(x, out_a, out_d, mod3, w_out_bf,
                                   g_post_mix[l].reshape(1, d), g_pre_ffn[l].reshape(1, d),
                                   w_gu_bf, w_down_bf, g_post_ffn[l].reshape(1, d)))
        xs = new_xs
    return tuple(xs)
```

```python
import functools
import math

import jax
import jax.numpy as jnp
from jax import lax
from jax.experimental import pallas as pl
from jax.experimental.pallas import tpu as pltpu

F32 = jnp.float32
BF16 = jnp.bfloat16

HEAD_DIM = 64
N_GQA_HEADS = 8
N_GQA_KV = 2
GQA_GROUP = N_GQA_HEADS // N_GQA_KV
N_DIFF_HEADS = 4
GRID_W = 64
NUM_BUCKETS = 32
MAX_DISTANCE = 128
ROPE_THETA = 10000.0
EPS = 1e-6
ATTN_SCALE = 1.0 / math.sqrt(HEAD_DIM)
LOG2E = 1.4426950408889634
Q_SCALE = ATTN_SCALE * LOG2E

LANES = 128
GQA_Q_COLS = N_GQA_HEADS * HEAD_DIM
GQA_KV_COLS = N_GQA_KV * HEAD_DIM
DIFF_COLS = N_DIFF_HEADS * 2 * HEAD_DIM

QA_BLK = 0
KA_BLK = QA_BLK + GQA_Q_COLS // LANES
VA_BLK = KA_BLK + N_GQA_KV
QD_BLK = VA_BLK + N_GQA_KV
KD_BLK = QD_BLK + N_DIFF_HEADS
VD_BLK = KD_BLK + N_DIFF_HEADS
ACT_COLS = (VD_BLK + N_DIFF_HEADS) * LANES

MOD_ROWS = 8
M_INIT = -0.5 * float(jnp.finfo(jnp.float32).max)

VMEM_LIMIT = 56 * 1024 * 1024
FLASH_UNROLL = 2
STRIP_ROWS = 32
BLOCK_ROWS = 512
FFN_GROUP_ROWS = 256


def _rms(x, g):
    return x * lax.rsqrt(jnp.mean(x * x, axis=-1, keepdims=True) + EPS) * g


def _mod_kernel(c_ref, w_ref, b_ref, o_ref):
    c = c_ref[...]
    a = c * (1.0 / (1.0 + jnp.exp(-c)))
    a_hi = a.astype(BF16)
    a_lo = (a - a_hi.astype(F32)).astype(BF16)
    w = w_ref[0]
    w_hi = w.astype(BF16)
    w_lo = (w - w_hi.astype(F32)).astype(BF16)
    acc = jnp.dot(a_hi, w_hi, preferred_element_type=F32)
    acc += jnp.dot(a_lo, w_hi, preferred_element_type=F32)
    acc += jnp.dot(a_hi, w_lo, preferred_element_type=F32)
    o_ref[...] = acc + b_ref[...]


def _modulation(c_all, w_ada, b_ada, layer):
    rows, d = c_all.shape
    n_out = w_ada.shape[2]
    tn = 1536
    return pl.pallas_call(
        _mod_kernel,
        grid=(n_out // tn,),
        in_specs=[
            pl.BlockSpec((rows, d), lambda j: (0, 0)),
            pl.BlockSpec((1, d, tn), lambda j: (layer, 0, j)),
            pl.BlockSpec((1, tn), lambda j: (0, j)),
        ],
        out_specs=pl.BlockSpec((rows, tn), lambda j: (0, j)),
        out_shape=jax.ShapeDtypeStruct((rows, n_out), F32),
        compiler_params=pltpu.CompilerParams(
            dimension_semantics=("arbitrary",), vmem_limit_bytes=VMEM_LIMIT),
        name="modulation",
    )(c_all, w_ada, b_ada.reshape(1, n_out))


def _inproj_kernel(x_ref, mod_ref, g_ref, w_ref, gq_ref, gk_ref, cos_ref, sin_ref, seg_ref,
                   o_ref):
    for r0 in range(0, x_ref.shape[1], FFN_GROUP_ROWS):
        _inproj_rows(slice(r0, r0 + FFN_GROUP_ROWS), x_ref, mod_ref, g_ref, w_ref, gq_ref,
                     gk_ref, cos_ref, sin_ref, seg_ref, o_ref)


def _inproj_rows(rows, x_ref, mod_ref, g_ref, w_ref, gq_ref, gk_ref, cos_ref, sin_ref, seg_ref,
                 o_ref):
    x = x_ref[0, rows, :]
    sh = mod_ref[0, 0:1, :]
    sc = mod_ref[0, 1:2, :]
    h = _rms(x, g_ref[...]) * (1.0 + sc) + sh
    proj = jnp.dot(h.astype(BF16), w_ref[...], preferred_element_type=F32)

    cos = cos_ref[rows, :]
    sin = sin_ref[rows, :]
    seg = seg_ref[...]
    lane = lax.broadcasted_iota(jnp.int32, (x.shape[0], LANES), 1)
    first_half = (lane & 16) == 0
    low = lane < HEAD_DIM

    def norm_rope(t, g):
        ss = jnp.dot((t * t).astype(BF16), seg, preferred_element_type=F32)
        tn = t * lax.rsqrt(ss * (1.0 / HEAD_DIM) + EPS) * g
        rot = jnp.where(first_half, pltpu.roll(tn, LANES - 16, 1), pltpu.roll(tn, 16, 1))
        return tn * cos + rot * sin

    def put(blk, val):
        o_ref[0, rows, blk * LANES:(blk + 1) * LANES] = val.astype(BF16)

    col = 0
    for j in range(GQA_Q_COLS // LANES):
        put(QA_BLK + j, norm_rope(proj[:, col:col + LANES], gq_ref[...]) * Q_SCALE)
        col += LANES
    kr = norm_rope(proj[:, col:col + LANES], gk_ref[...])
    col += LANES
    kr_sw = pltpu.roll(kr, HEAD_DIM, 1)
    put(KA_BLK + 0, jnp.where(low, kr, kr_sw))
    put(KA_BLK + 1, jnp.where(low, kr_sw, kr))
    va = proj[:, col:col + LANES]
    col += LANES
    va_sw = pltpu.roll(va, HEAD_DIM, 1)
    put(VA_BLK + 0, jnp.where(low, va, 1.0))
    put(VA_BLK + 1, jnp.where(low, va_sw, 1.0))
    for j in range(N_DIFF_HEADS):
        put(QD_BLK + j, proj[:, col:col + LANES] * Q_SCALE)
        col += LANES
    for j in range(N_DIFF_HEADS):
        put(KD_BLK + j, proj[:, col:col + LANES])
        col += LANES
    for j in range(N_DIFF_HEADS):
        put(VD_BLK + j, proj[:, col:col + LANES])
        col += LANES


def _in_projection(x, mod3, g_pre, w_in_bf, gq2, gk2, cos2, sin2, seg, tm=512):
    b, n, d = x.shape
    in_w = w_in_bf.shape[1]
    const = lambda bi, i: (0, 0)
    return pl.pallas_call(
        _inproj_kernel,
        grid=(b, n // tm),
        in_specs=[
            pl.BlockSpec((1, tm, d), lambda bi, i: (bi, i, 0)),
            pl.BlockSpec((1, 6, d), lambda bi, i: (bi, 0, 0)),
            pl.BlockSpec((1, d), const),
            pl.BlockSpec((d, in_w), const),
            pl.BlockSpec((1, LANES), const),
            pl.BlockSpec((1, LANES), const),
            pl.BlockSpec((tm, LANES), lambda bi, i: (i, 0)),
            pl.BlockSpec((tm, LANES), lambda bi, i: (i, 0)),
            pl.BlockSpec((LANES, LANES), const),
        ],
        out_specs=pl.BlockSpec((1, tm, ACT_COLS), lambda bi, i: (bi, i, 0)),
        out_shape=jax.ShapeDtypeStruct((b, n, ACT_COLS), BF16),
        compiler_params=pltpu.CompilerParams(
            dimension_semantics=("arbitrary", "arbitrary"), vmem_limit_bytes=VMEM_LIMIT),
        name="in_projection",
    )(x, mod3, g_pre, w_in_bf, gq2, gk2, cos2, sin2, seg)


def _flash_sweep(nc, tk, qz_ref, k_ref, v_ref, blocks, bufs, m_ref, acc_ref, *, v_ones=None,
                 n_tail=0, main_chunk=None, main_shift=None, tail_chunk=None, tail_bias=None,
                 max_unroll=FLASH_UNROLL):
    n_main = nc - n_tail
    unroll = min(max_unroll, n_main) if n_main else 0
    assert n_tail % 2 == 0 and (n_main == 0 or (unroll % 2 == 0 and n_main % unroll == 0))

    def chunk_of(kind, idx):
        if kind == "tail":
            return tail_chunk(idx), None, tail_bias
        c = idx if main_chunk is None else main_chunk(idx)
        return c, (None if main_shift is None else main_shift(c)), None

    def scores(kind, idx, s_ref, mc_ref):
        c, shift, bias = chunk_of(kind, idx)
        start = pl.multiple_of(c * tk, tk)
        kc = k_ref[0, pl.ds(start, tk), :]
        for rows in blocks:
            s = lax.dot_general(qz_ref[rows, :], kc, (((1,), (1,)), ((), ())),
                                preferred_element_type=F32)
            if bias is not None:
                s = s + bias(c, rows)
            s_ref[rows, :] = s
            mc = jnp.max(s, axis=-1, keepdims=True)
            if shift is not None:
                mc = mc + shift
            mc_ref[rows, :] = jnp.broadcast_to(mc, (s.shape[0], LANES))

    def values(kind, idx, s_ref, mc_ref):
        c, shift, _ = chunk_of(kind, idx)
        start = pl.multiple_of(c * tk, tk)
        vc = v_ref[0, pl.ds(start, tk), :]
        if v_ones is not None:
            vc = jnp.concatenate([vc, v_ones], axis=1)
        reps = acc_ref.shape[1] // LANES
        for rows in blocks:
            strips = []
            for r0 in range(rows.start, rows.stop, STRIP_ROWS):
                rs = slice(r0, r0 + STRIP_ROWS)
                m_prev = m_ref[rs, :]
                m_new = jnp.maximum(m_prev, mc_ref[rs, :])
                alpha = jnp.exp2(m_prev - m_new)
                m_ref[rs, :] = m_new
                for j in range(reps):
                    cols = slice(j * LANES, (j + 1) * LANES)
                    acc_ref[rs, cols] = alpha * acc_ref[rs, cols]
                m_sub = m_new if shift is None else m_new - shift
                strips.append(jnp.concatenate(
                    [jnp.exp2(s_ref[rs, j * LANES:(j + 1) * LANES] - m_sub).astype(BF16)
                     for j in range(tk // LANES)], axis=1))
            p = jnp.concatenate(strips, axis=0)
            acc_ref[rows, :] += jnp.dot(p, vc, preferred_element_type=F32)

    def group(kind, base, count, nxt):
        for u in range(count):
            if u + 1 < count:
                scores(kind, base + u + 1, *bufs[(u + 1) % 2])
            elif nxt is not None:
                scores(*nxt, *bufs[(u + 1) % 2])
            values(kind, base + u, *bufs[u % 2])

    tail_start = ("tail", 0) if n_tail else None
    scores(*(("main", 0) if n_main else tail_start), *bufs[0])
    if n_main:
        n_groups = n_main // unroll

        def body(g, carry):
            group("main", g * unroll, unroll, ("main", (g + 1) * unroll))
            return carry

        lax.fori_loop(0, n_groups - 1, body, 0)
        group("main", n_main - unroll, unroll, tail_start)
    if n_tail:
        group("tail", 0, n_tail, None)


def _flash_scratch(mrows, tk, acc_cols):
    return [
        pltpu.VMEM((mrows, LANES), BF16),
        pltpu.VMEM((mrows, LANES), F32),
        pltpu.VMEM((mrows, acc_cols), F32),
        pltpu.VMEM((mrows, tk), F32),
        pltpu.VMEM((mrows, tk), F32),
        pltpu.VMEM((mrows, LANES), F32),
        pltpu.VMEM((mrows, LANES), F32),
    ]


def _gqa_kernel(q_ref, k_ref, v_ref, o_ref, qz_ref, m_ref, acc_ref, s0_ref, s1_ref,
                mc0_ref, mc1_ref, *, tq, tk):
    n = k_ref.shape[1]
    mrows = GQA_GROUP * tq
    lane = lax.broadcasted_iota(jnp.int32, (tq, LANES), 1)
    low = lane < HEAD_DIM
    for g in range(GQA_GROUP):
        slab = q_ref[0, :, (g // 2) * LANES:(g // 2 + 1) * LANES]
        keep = low if g % 2 == 0 else jnp.logical_not(low)
        qz_ref[g * tq:(g + 1) * tq, :] = jnp.where(keep, slab, jnp.zeros_like(slab))
    m_ref[...] = jnp.full(m_ref.shape, M_INIT, F32)
    acc_ref[...] = jnp.zeros(acc_ref.shape, F32)

    blocks = tuple(slice(r, r + BLOCK_ROWS) for r in range(0, mrows, BLOCK_ROWS))
    _flash_sweep(n // tk, tk, qz_ref, k_ref, v_ref, blocks,
                 ((s0_ref, mc0_ref), (s1_ref, mc1_ref)), m_ref, acc_ref,
                 max_unroll=2 * FLASH_UNROLL)

    outs = []
    for g in range(GQA_GROUP):
        a = acc_ref[g * tq:(g + 1) * tq, :]
        outs.append(a / pltpu.roll(a, HEAD_DIM, 1))
    for j in range(GQA_GROUP // 2):
        pair = jnp.where(low, outs[2 * j], pltpu.roll(outs[2 * j + 1], HEAD_DIM, 1))
        o_ref[0, :, j * LANES:(j + 1) * LANES] = pair.astype(o_ref.dtype)


def _gqa_attention(act, tq=512, tk=1024):
    b, n, _ = act.shape
    width = GQA_GROUP * HEAD_DIM
    mrows = GQA_GROUP * tq
    return pl.pallas_call(
        functools.partial(_gqa_kernel, tq=tq, tk=tk),
        grid=(b, N_GQA_KV, n // tq),
        in_specs=[
            pl.BlockSpec((1, tq, width), lambda bi, h, i: (bi, i, h)),
            pl.BlockSpec((1, n, LANES), lambda bi, h, i: (bi, 0, KA_BLK + h)),
            pl.BlockSpec((1, n, LANES), lambda bi, h, i: (bi, 0, VA_BLK + h)),
        ],
        out_specs=pl.BlockSpec((1, tq, width), lambda bi, h, i: (bi, i, h)),
        out_shape=jax.ShapeDtypeStruct((b, n, GQA_Q_COLS), BF16),
        scratch_shapes=_flash_scratch(mrows, tk, LANES),
        compiler_params=pltpu.CompilerParams(
            dimension_semantics=("arbitrary", "arbitrary", "arbitrary"),
            vmem_limit_bytes=VMEM_LIMIT),
        name="gqa_attention",
    )(act, act, act)


def _rel_bucket(rel):
    half = NUM_BUCKETS // 2
    max_exact = half // 2
    n = jnp.minimum(jnp.abs(rel), MAX_DISTANCE)
    n2 = n * n
    large = jnp.full(rel.shape, max_exact, jnp.int32)
    for k in range(1, half - max_exact):
        large = large + (n2 >= (max_exact * max_exact) * (2 ** k)).astype(jnp.int32)
    return jnp.where(rel > 0, half, 0) + jnp.where(n < max_exact, n, large)


def _bias_geometry(tb, tk, n_sub):
    assert tk % tb == 0
    ratio = tk // tb
    e_lo = (-(MAX_DISTANCE - 1) - tk) // tb
    e_hi = -((-(MAX_DISTANCE - 1) - tb) // tb)
    spans = [-(-(j + n_sub - 1 + e_hi) // ratio) - 1 - (j + e_lo) // ratio
             for j in range(0, ratio * n_sub, n_sub)]
    return e_lo, e_hi, max(spans) + max(spans) % 2


def _diff_kernel(rb_ref, lam_ref, gs_ref, q_ref, k_ref, v_ref, o_ref,
                 qz_ref, m_ref, acc_ref, s0_ref, s1_ref, mc0_ref, mc1_ref, bias_ref,
                 *, t, tk, lam_init):
    h = pl.program_id(0)
    bi = pl.program_id(1)
    i = pl.program_id(2)
    n = k_ref.shape[1]
    nc = n // tk
    tb = BLOCK_ROWS
    n_sub = t // tb
    ratio = tk // tb
    e_lo, e_hi, near_chunks = _bias_geometry(tb, tk, n_sub)
    shift_before = rb_ref[NUM_BUCKETS // 2 - 1, h] * LOG2E
    shift_after = rb_ref[NUM_BUCKETS - 1, h] * LOG2E

    @pl.when(jnp.logical_and(bi == 0, i == 0))
    def _():
        width = (e_hi - 1 - e_lo) * tb + tk
        bucket = _rel_bucket(e_lo * tb + lax.broadcasted_iota(jnp.int32, (8, width), 1))
        gen = jnp.zeros((8, width), F32)
        for bkt in range(NUM_BUCKETS):
            gen = jnp.where(bucket == bkt, rb_ref[bkt, h] * LOG2E, gen)
        sub = lax.broadcasted_iota(jnp.int32, (8, width), 0)
        rows8 = gen
        for r in range(1, 8):
            rows8 = jnp.where(sub == r, pltpu.roll(gen, r, 1), rows8)
        for r0 in range(0, tb, 8):
            blk = rows8 if r0 == 0 else pltpu.roll(rows8, r0, 1)
            for e in range(e_lo + 1, e_hi):
                off = (e - e_lo) * tb
                bias_ref[e - e_lo, r0:r0 + 8, :] = blk[:, off:off + tk]
        bias_ref[0] = jnp.full((tb, tk), shift_before, F32)
        bias_ref[e_hi - e_lo] = jnp.full((tb, tk), shift_after, F32)

    lane = lax.broadcasted_iota(jnp.int32, (t, LANES), 1)
    low = lane < HEAD_DIM
    q = q_ref[0]
    zero = jnp.zeros_like(q)
    qz_ref[0:t, :] = jnp.where(low, q, zero)
    qz_ref[t:2 * t, :] = jnp.where(low, zero, q)
    m_ref[...] = jnp.full(m_ref.shape, M_INIT, F32)
    acc_ref[...] = jnp.zeros(acc_ref.shape, F32)

    n_near = min(near_chunks, nc)
    near0 = jnp.clip(jnp.floor_divide(i * n_sub + e_lo, ratio) + 1, 0, nc - n_near)

    def tail_bias(c, rows):
        block = i * n_sub + (rows.start % t) // tb
        return bias_ref[jnp.clip(ratio * c - block, e_lo, e_hi) - e_lo]

    _flash_sweep(
        nc, tk, qz_ref, k_ref, v_ref,
        tuple(slice(r, r + BLOCK_ROWS) for r in range(0, 2 * t, BLOCK_ROWS)),
        ((s0_ref, mc0_ref), (s1_ref, mc1_ref)), m_ref, acc_ref,
        v_ones=jnp.ones((tk, LANES), BF16), n_tail=n_near,
        main_chunk=lambda pos: pos + jnp.where(pos >= near0, n_near, 0),
        main_shift=lambda c: jnp.where(c < near0, shift_before, shift_after),
        tail_chunk=lambda u: near0 + u,
        tail_bias=tail_bias)

    lv = lam_ref[...]
    lam = (jnp.exp(jnp.sum(lv[0:1, :] * lv[1:2, :], axis=-1, keepdims=True))
           - jnp.exp(jnp.sum(lv[2:3, :] * lv[3:4, :], axis=-1, keepdims=True)) + lam_init)
    o1 = acc_ref[0:t, 0:LANES] / acc_ref[0:t, LANES:2 * LANES]
    o2 = acc_ref[t:2 * t, 0:LANES] / acc_ref[t:2 * t, LANES:2 * LANES]
    o = o1 - lam * o2
    o_ref[0] = (_rms(o, gs_ref[...]) * (1.0 - lam_init)).astype(o_ref.dtype)


def _diff_attention(act, rel_bias, lam_rows, g_subln, lam_init, t=1024, tk=1024):
    b, n, _ = act.shape
    e_lo, e_hi, _ = _bias_geometry(BLOCK_ROWS, tk, t // BLOCK_ROWS)
    return pl.pallas_call(
        functools.partial(_diff_kernel, t=t, tk=tk, lam_init=lam_init),
        grid=(N_DIFF_HEADS, b, n // t),
        in_specs=[
            pl.BlockSpec(memory_space=pltpu.SMEM),
            pl.BlockSpec((MOD_ROWS, LANES), lambda h, bi, i: (0, 0)),
            pl.BlockSpec((1, LANES), lambda h, bi, i: (0, 0)),
            pl.BlockSpec((1, t, LANES), lambda h, bi, i: (bi, i, QD_BLK + h)),
            pl.BlockSpec((1, n, LANES), lambda h, bi, i: (bi, 0, KD_BLK + h),
                         pipeline_mode=pl.Buffered(1)),
            pl.BlockSpec((1, n, LANES), lambda h, bi, i: (bi, 0, VD_BLK + h),
                         pipeline_mode=pl.Buffered(1)),
        ],
        out_specs=pl.BlockSpec((1, t, LANES), lambda h, bi, i: (bi, i, h)),
        out_shape=jax.ShapeDtypeStruct((b, n, DIFF_COLS), BF16),
        scratch_shapes=_flash_scratch(2 * t, tk, 2 * LANES)
        + [pltpu.VMEM((e_hi - e_lo + 1, BLOCK_ROWS, tk), F32)],
        compiler_params=pltpu.CompilerParams(
            dimension_semantics=("arbitrary", "arbitrary", "arbitrary"),
            vmem_limit_bytes=VMEM_LIMIT),
        name="diff_attention",
    )(rel_bias, lam_rows, g_subln, act, act, act)


def _ffn_kernel(x_ref, oa_ref, od_ref, mod_ref, wout_ref, gpm_ref, gpf_ref, wgu_ref, wdn_ref,
                gpo_ref, y_ref):
    gt1 = mod_ref[0, 2:3, :]
    sh2 = mod_ref[0, 3:4, :]
    sc2 = mod_ref[0, 4:5, :]
    gt2 = mod_ref[0, 5:6, :]
    wa = oa_ref.shape[2]
    d_ff = wdn_ref.shape[0]
    tm = x_ref.shape[1]
    groups = [slice(r0, r0 + FFN_GROUP_ROWS) for r0 in range(0, tm, FFN_GROUP_ROWS)]
    mixes = [jnp.dot(oa_ref[0, rows, :], wout_ref[0:wa, :], preferred_element_type=F32)
             + jnp.dot(od_ref[0, rows, :], wout_ref[wa:, :], preferred_element_type=F32)
             for rows in groups]
    x1s, gus, fs = [], [], []
    for rows, mix in zip(groups, mixes):
        x1 = x_ref[0, rows, :] + gt1 * _rms(mix, gpm_ref[...])
        h = _rms(x1, gpf_ref[...]) * (1.0 + sc2) + sh2
        x1s.append(x1)
        gus.append(jnp.dot(h.astype(BF16), wgu_ref[...], preferred_element_type=F32))
    for gu in gus:
        gate = gu[:, :d_ff]
        up = gu[:, d_ff:]
        act = gate * (1.0 / (1.0 + jnp.exp(-gate))) * up
        fs.append(jnp.dot(act.astype(BF16), wdn_ref[...], preferred_element_type=F32))
    for rows, x1, f in zip(groups, x1s, fs):
        y_ref[0, rows, :] = x1 + gt2 * _rms(f, gpo_ref[...])


def _out_ffn(x, out_a, out_d, mod3, w_out_bf, g_post_mix, g_pre_ffn, w_gu_bf, w_down_bf,
             g_post_ffn, tm=512):
    b, n, d = x.shape
    const = lambda bi, i: (0, 0)
    once = pl.Buffered(1)
    return pl.pallas_call(
        _ffn_kernel,
        grid=(b, n // tm),
        in_specs=[
            pl.BlockSpec((1, tm, d), lambda bi, i: (bi, i, 0)),
            pl.BlockSpec((1, tm, out_a.shape[2]), lambda bi, i: (bi, i, 0)),
            pl.BlockSpec((1, tm, out_d.shape[2]), lambda bi, i: (bi, i, 0)),
            pl.BlockSpec((1, 6, d), lambda bi, i: (bi, 0, 0)),
            pl.BlockSpec(w_out_bf.shape, const, pipeline_mode=once),
            pl.BlockSpec((1, d), const),
            pl.BlockSpec((1, d), const),
            pl.BlockSpec(w_gu_bf.shape, const, pipeline_mode=once),
            pl.BlockSpec(w_down_bf.shape, const, pipeline_mode=once),
            pl.BlockSpec((1, d), const),
        ],
        out_specs=pl.BlockSpec((1, tm, d), lambda bi, i: (bi, i, 0)),
        out_shape=jax.ShapeDtypeStruct((b, n, d), F32),
        compiler_params=pltpu.CompilerParams(
            dimension_semantics=("arbitrary", "arbitrary"), vmem_limit_bytes=VMEM_LIMIT),
        name="out_ffn",
    )(x, out_a, out_d, mod3, w_out_bf, g_post_mix, g_pre_ffn, w_gu_bf, w_down_bf, g_post_ffn)


def _rope_tables(n):
    rows = n // GRID_W
    half = HEAD_DIM // 2
    quarter = half // 2
    lane = jnp.arange(LANES)
    inv_lane = (ROPE_THETA ** (-(2 * (lane % quarter)).astype(F32) / half))[None, :]
    ang_row = jnp.arange(rows, dtype=F32)[:, None] * inv_lane
    ang_col = jnp.arange(GRID_W, dtype=F32)[:, None] * inv_lane
    is_row = (lane % HEAD_DIM < half)[None, None, :]
    sign = jnp.where(lane % half < quarter, -1.0, 1.0).astype(F32)

    def table(fn, scale):
        by_row = (fn(ang_row) * scale)[:, None, :]
        by_col = (fn(ang_col) * scale)[None, :, :]
        return jnp.where(is_row, by_row, by_col).reshape(n, LANES)

    return table(jnp.cos, 1.0), table(jnp.sin, sign[None, :])


def kernel(x_prompt, x_sample, c_prompt, c_sample, rel_bias, w_ada, b_ada, g_pre_mix, w_in,
           g_q, g_k, lam_q1, lam_k1, lam_q2, lam_k2, g_subln, w_out, g_post_mix, g_pre_ffn,
           w_gu, w_down, g_post_ffn):
    depth = w_ada.shape[0]
    d = x_prompt.shape[-1]
    xs = [x_prompt, x_sample]
    cs = [c_prompt, c_sample]
    n_c = sum(c.shape[0] for c in cs)
    assert n_c <= MOD_ROWS
    c_all = jnp.concatenate(cs + [jnp.zeros((MOD_ROWS - n_c, d), F32)], axis=0)
    seg = (jnp.arange(LANES)[:, None] // HEAD_DIM
           == jnp.arange(LANES)[None, :] // HEAD_DIM).astype(BF16)
    tables = [_rope_tables(max(x.shape[1] for x in xs))] * len(xs)

    for l in range(depth):
        lam_init = 0.8 - 0.6 * math.exp(-0.3 * l)
        mod = _modulation(c_all, w_ada, b_ada[l], l).reshape(MOD_ROWS, 6, d)
        w_in_bf = w_in[l].astype(BF16)
        w_out_bf = w_out[l].astype(BF16)
        w_gu_bf = w_gu[l].astype(BF16)
        w_down_bf = w_down[l].astype(BF16)
        gq2 = jnp.tile(g_q[l], LANES // HEAD_DIM).reshape(1, LANES)
        gk2 = jnp.tile(g_k[l], LANES // HEAD_DIM).reshape(1, LANES)
        lam_rows = jnp.zeros((MOD_ROWS, LANES), F32).at[0:4, 0:HEAD_DIM].set(
            jnp.stack([lam_q1[l], lam_k1[l], lam_q2[l], lam_k2[l]]))
        new_xs = []
        row0 = 0
        for x, (cos2, sin2) in zip(xs, tables):
            mod3 = mod[row0:row0 + x.shape[0]]
            row0 += x.shape[0]
            act = _in_projection(x, mod3, g_pre_mix[l].reshape(1, d), w_in_bf, gq2, gk2,
                                 cos2, sin2, seg)
            out_a = _gqa_attention(act)
            out_d = _diff_attention(act, rel_bias, lam_rows, g_subln[l].reshape(1, LANES),
                                    lam_init)
            new_xs.append(_out_ffn(x, out_a, out_d, mod3, w_out_bf,
                                   g_post_mix[l].reshape(1, d), g_pre_ffn[l].reshape(1, d),
                                   w_gu_bf, w_down_bf, g_post_ffn[l].reshape(1, d)))
        xs = new_xs
    return tuple(xs)
```

```python
import functools
import math

import jax
import jax.numpy as jnp
from jax import lax
from jax.experimental import pallas as pl
from jax.experimental.pallas import tpu as pltpu

F32 = jnp.float32
BF16 = jnp.bfloat16

HEAD_DIM = 64
N_GQA_HEADS = 8
N_GQA_KV = 2
GQA_GROUP = N_GQA_HEADS // N_GQA_KV
N_DIFF_HEADS = 4
GRID_W = 64
NUM_BUCKETS = 32
MAX_DISTANCE = 128
ROPE_THETA = 10000.0
EPS = 1e-6
ATTN_SCALE = 1.0 / math.sqrt(HEAD_DIM)
LOG2E = 1.4426950408889634
Q_SCALE = ATTN_SCALE * LOG2E

LANES = 128
GQA_Q_COLS = N_GQA_HEADS * HEAD_DIM
GQA_KV_COLS = N_GQA_KV * HEAD_DIM
DIFF_COLS = N_DIFF_HEADS * 2 * HEAD_DIM

QA_BLK = 0
KA_BLK = QA_BLK + GQA_Q_COLS // LANES
VA_BLK = KA_BLK + N_GQA_KV
QD_BLK = VA_BLK + N_GQA_KV
KD_BLK = QD_BLK + N_DIFF_HEADS
VD_BLK = KD_BLK + N_DIFF_HEADS
ACT_COLS = (VD_BLK + N_DIFF_HEADS) * LANES

MOD_ROWS = 8
M_INIT = -0.5 * float(jnp.finfo(jnp.float32).max)

VMEM_LIMIT = 56 * 1024 * 1024
FLASH_UNROLL = 2
STRIP_ROWS = 32
BLOCK_ROWS = 512
FFN_GROUP_ROWS = 256


def _rms(x, g):
    return x * lax.rsqrt(jnp.mean(x * x, axis=-1, keepdims=True) + EPS) * g


def _mod_kernel(c_ref, w_ref, b_ref, o_ref):
    c = c_ref[...]
    a = c * (1.0 / (1.0 + jnp.exp(-c)))
    a_hi = a.astype(BF16)
    a_lo = (a - a_hi.astype(F32)).astype(BF16)
    w = w_ref[0]
    w_hi = w.astype(BF16)
    w_lo = (w - w_hi.astype(F32)).astype(BF16)
    acc = jnp.dot(a_hi, w_hi, preferred_element_type=F32)
    acc += jnp.dot(a_lo, w_hi, preferred_element_type=F32)
    acc += jnp.dot(a_hi, w_lo, preferred_element_type=F32)
    o_ref[...] = acc + b_ref[...]


def _modulation(c_all, w_ada, b_ada, layer):
    rows, d = c_all.shape
    n_out = w_ada.shape[2]
    tn = 1536
    return pl.pallas_call(
        _mod_kernel,
        grid=(n_out // tn,),
        in_specs=[
            pl.BlockSpec((rows, d), lambda j: (0, 0)),
            pl.BlockSpec((1, d, tn), lambda j: (layer, 0, j)),
            pl.BlockSpec((1, tn), lambda j: (0, j)),
        ],
        out_specs=pl.BlockSpec((rows, tn), lambda j: (0, j)),
        out_shape=jax.ShapeDtypeStruct((rows, n_out), F32),
        compiler_params=pltpu.CompilerParams(
            dimension_semantics=("arbitrary",), vmem_limit_bytes=VMEM_LIMIT),
        name="modulation",
    )(c_all, w_ada, b_ada.reshape(1, n_out))


def _inproj_kernel(x_ref, mod_ref, g_ref, w_ref, gq_ref, gk_ref, cos_ref, sin_ref, seg_ref,
                   o_ref):
    for r0 in range(0, x_ref.shape[1], FFN_GROUP_ROWS):
        _inproj_rows(slice(r0, r0 + FFN_GROUP_ROWS), x_ref, mod_ref, g_ref, w_ref, gq_ref,
                     gk_ref, cos_ref, sin_ref, seg_ref, o_ref)


def _inproj_rows(rows, x_ref, mod_ref, g_ref, w_ref, gq_ref, gk_ref, cos_ref, sin_ref, seg_ref,
                 o_ref):
    x = x_ref[0, rows, :]
    sh = mod_ref[0, 0:1, :]
    sc = mod_ref[0, 1:2, :]
    h = _rms(x, g_ref[...]) * (1.0 + sc) + sh
    proj = jnp.dot(h.astype(BF16), w_ref[...], preferred_element_type=F32)

    cos = cos_ref[rows, :]
    sin = sin_ref[rows, :]
    seg = seg_ref[...]
    lane = lax.broadcasted_iota(jnp.int32, (x.shape[0], LANES), 1)
    first_half = (lane & 16) == 0
    low = lane < HEAD_DIM

    def norm_rope(t, g):
        ss = jnp.dot((t * t).astype(BF16), seg, preferred_element_type=F32)
        tn = t * lax.rsqrt(ss * (1.0 / HEAD_DIM) + EPS) * g
        rot = jnp.where(first_half, pltpu.roll(tn, LANES - 16, 1), pltpu.roll(tn, 16, 1))
        return tn * cos + rot * sin

    def put(blk, val):
        o_ref[0, rows, blk * LANES:(blk + 1) * LANES] = val.astype(BF16)

    col = 0
    for j in range(GQA_Q_COLS // LANES):
        put(QA_BLK + j, norm_rope(proj[:, col:col + LANES], gq_ref[...]) * Q_SCALE)
        col += LANES
    kr = norm_rope(proj[:, col:col + LANES], gk_ref[...])
    col += LANES
    kr_sw = pltpu.roll(kr, HEAD_DIM, 1)
    put(KA_BLK + 0, jnp.where(low, kr, kr_sw))
    put(KA_BLK + 1, jnp.where(low, kr_sw, kr))
    va = proj[:, col:col + LANES]
    col += LANES
    va_sw = pltpu.roll(va, HEAD_DIM, 1)
    put(VA_BLK + 0, jnp.where(low, va, 1.0))
    put(VA_BLK + 1, jnp.where(low, va_sw, 1.0))
    for j in range(N_DIFF_HEADS):
        put(QD_BLK + j, proj[:, col:col + LANES] * Q_SCALE)
        col += LANES
    for j in range(N_DIFF_HEADS):
        put(KD_BLK + j, proj[:, col:col + LANES])
        col += LANES
    for j in range(N_DIFF_HEADS):
        put(VD_BLK + j, proj[:, col:col + LANES])
        col += LANES


def _in_projection(x, mod3, g_pre, w_in_bf, gq2, gk2, cos2, sin2, seg, tm=1024):
    b, n, d = x.shape
    in_w = w_in_bf.shape[1]
    const = lambda bi, i: (0, 0)
    return pl.pallas_call(
        _inproj_kernel,
        grid=(b, n // tm),
        in_specs=[
            pl.BlockSpec((1, tm, d), lambda bi, i: (bi, i, 0)),
            pl.BlockSpec((1, 6, d), lambda bi, i: (bi, 0, 0)),
            pl.BlockSpec((1, d), const),
            pl.BlockSpec((d, in_w), const),
            pl.BlockSpec((1, LANES), const),
            pl.BlockSpec((1, LANES), const),
            pl.BlockSpec((tm, LANES), lambda bi, i: (i, 0)),
            pl.BlockSpec((tm, LANES), lambda bi, i: (i, 0)),
            pl.BlockSpec((LANES, LANES), const),
        ],
        out_specs=pl.BlockSpec((1, tm, ACT_COLS), lambda bi, i: (bi, i, 0)),
        out_shape=jax.ShapeDtypeStruct((b, n, ACT_COLS), BF16),
        compiler_params=pltpu.CompilerParams(
            dimension_semantics=("arbitrary", "arbitrary"), vmem_limit_bytes=VMEM_LIMIT),
        name="in_projection",
    )(x, mod3, g_pre, w_in_bf, gq2, gk2, cos2, sin2, seg)


def _flash_sweep(nc, tk, qz_ref, k_ref, v_ref, blocks, bufs, m_ref, acc_ref, *, v_ones=None,
                 n_tail=0, main_chunk=None, main_shift=None, tail_chunk=None, tail_bias=None,
                 max_unroll=FLASH_UNROLL):
    n_main = nc - n_tail
    unroll = min(max_unroll, n_main) if n_main else 0
    assert n_tail % 2 == 0 and (n_main == 0 or (unroll % 2 == 0 and n_main % unroll == 0))

    def chunk_of(kind, idx):
        if kind == "tail":
            return tail_chunk(idx), None, tail_bias
        c = idx if main_chunk is None else main_chunk(idx)
        return c, (None if main_shift is None else main_shift(c)), None

    def scores(kind, idx, s_ref, mc_ref):
        c, shift, bias = chunk_of(kind, idx)
        start = pl.multiple_of(c * tk, tk)
        kc = k_ref[0, pl.ds(start, tk), :]
        for rows in blocks:
            s = lax.dot_general(qz_ref[rows, :], kc, (((1,), (1,)), ((), ())),
                                preferred_element_type=F32)
            if bias is not None:
                s = s + bias(c, rows)
            s_ref[rows, :] = s
            mc = jnp.max(s, axis=-1, keepdims=True)
            if shift is not None:
                mc = mc + shift
            mc_ref[rows, :] = jnp.broadcast_to(mc, (s.shape[0], LANES))

    def values(kind, idx, s_ref, mc_ref):
        c, shift, _ = chunk_of(kind, idx)
        start = pl.multiple_of(c * tk, tk)
        vc = v_ref[0, pl.ds(start, tk), :]
        if v_ones is not None:
            vc = jnp.concatenate([vc, v_ones], axis=1)
        reps = acc_ref.shape[1] // LANES
        for rows in blocks:
            strips = []
            for r0 in range(rows.start, rows.stop, STRIP_ROWS):
                rs = slice(r0, r0 + STRIP_ROWS)
                m_prev = m_ref[rs, :]
                m_new = jnp.maximum(m_prev, mc_ref[rs, :])
                alpha = jnp.exp2(m_prev - m_new)
                m_ref[rs, :] = m_new
                for j in range(reps):
                    cols = slice(j * LANES, (j + 1) * LANES)
                    acc_ref[rs, cols] = alpha * acc_ref[rs, cols]
                m_sub = m_new if shift is None else m_new - shift
                strips.append(jnp.concatenate(
                    [jnp.exp2(s_ref[rs, j * LANES:(j + 1) * LANES] - m_sub).astype(BF16)
                     for j in range(tk // LANES)], axis=1))
            p = jnp.concatenate(strips, axis=0)
            acc_ref[rows, :] += jnp.dot(p, vc, preferred_element_type=F32)

    def group(kind, base, count, nxt):
        for u in range(count):
            if u + 1 < count:
                scores(kind, base + u + 1, *bufs[(u + 1) % 2])
            elif nxt is not None:
                scores(*nxt, *bufs[(u + 1) % 2])
            values(kind, base + u, *bufs[u % 2])

    tail_start = ("tail", 0) if n_tail else None
    scores(*(("main", 0) if n_main else tail_start), *bufs[0])
    if n_main:
        n_groups = n_main // unroll

        def body(g, carry):
            group("main", g * unroll, unroll, ("main", (g + 1) * unroll))
            return carry

        lax.fori_loop(0, n_groups - 1, body, 0)
        group("main", n_main - unroll, unroll, tail_start)
    if n_tail:
        group("tail", 0, n_tail, None)


def _flash_scratch(mrows, tk, acc_cols):
    return [
        pltpu.VMEM((mrows, LANES), BF16),
        pltpu.VMEM((mrows, LANES), F32),
        pltpu.VMEM((mrows, acc_cols), F32),
        pltpu.VMEM((mrows, tk), F32),
        pltpu.VMEM((mrows, tk), F32),
        pltpu.VMEM((mrows, LANES), F32),
        pltpu.VMEM((mrows, LANES), F32),
    ]


def _gqa_kernel(q_ref, k_ref, v_ref, o_ref, qz_ref, m_ref, acc_ref, s0_ref, s1_ref,
                mc0_ref, mc1_ref, *, tq, tk):
    n = k_ref.shape[1]
    mrows = GQA_GROUP * tq
    lane = lax.broadcasted_iota(jnp.int32, (tq, LANES), 1)
    low = lane < HEAD_DIM
    for g in range(GQA_GROUP):
        slab = q_ref[0, :, (g // 2) * LANES:(g // 2 + 1) * LANES]
        keep = low if g % 2 == 0 else jnp.logical_not(low)
        qz_ref[g * tq:(g + 1) * tq, :] = jnp.where(keep, slab, jnp.zeros_like(slab))
    m_ref[...] = jnp.full(m_ref.shape, M_INIT, F32)
    acc_ref[...] = jnp.zeros(acc_ref.shape, F32)

    blocks = tuple(slice(r, r + BLOCK_ROWS) for r in range(0, mrows, BLOCK_ROWS))
    _flash_sweep(n // tk, tk, qz_ref, k_ref, v_ref, blocks,
                 ((s0_ref, mc0_ref), (s1_ref, mc1_ref)), m_ref, acc_ref,
                 max_unroll=2 * FLASH_UNROLL)

    outs = []
    for g in range(GQA_GROUP):
        a = acc_ref[g * tq:(g + 1) * tq, :]
        outs.append(a / pltpu.roll(a, HEAD_DIM, 1))
    for j in range(GQA_GROUP // 2):
        pair = jnp.where(low, outs[2 * j], pltpu.roll(outs[2 * j + 1], HEAD_DIM, 1))
        o_ref[0, :, j * LANES:(j + 1) * LANES] = pair.astype(o_ref.dtype)


def _gqa_attention(act, tq=512, tk=1024):
    b, n, _ = act.shape
    width = GQA_GROUP * HEAD_DIM
    mrows = GQA_GROUP * tq
    return pl.pallas_call(
        functools.partial(_gqa_kernel, tq=tq, tk=tk),
        grid=(b, N_GQA_KV, n // tq),
        in_specs=[
            pl.BlockSpec((1, tq, width), lambda bi, h, i: (bi, i, h)),
            pl.BlockSpec((1, n, LANES), lambda bi, h, i: (bi, 0, KA_BLK + h)),
            pl.BlockSpec((1, n, LANES), lambda bi, h, i: (bi, 0, VA_BLK + h)),
        ],
        out_specs=pl.BlockSpec((1, tq, width), lambda bi, h, i: (bi, i, h)),
        out_shape=jax.ShapeDtypeStruct((b, n, GQA_Q_COLS), BF16),
        scratch_shapes=_flash_scratch(mrows, tk, LANES),
        compiler_params=pltpu.CompilerParams(
            dimension_semantics=("arbitrary", "arbitrary", "arbitrary"),
            vmem_limit_bytes=VMEM_LIMIT),
        name="gqa_attention",
    )(act, act, act)


def _rel_bucket(rel):
    half = NUM_BUCKETS // 2
    max_exact = half // 2
    n = jnp.minimum(jnp.abs(rel), MAX_DISTANCE)
    n2 = n * n
    large = jnp.full(rel.shape, max_exact, jnp.int32)
    for k in range(1, half - max_exact):
        large = large + (n2 >= (max_exact * max_exact) * (2 ** k)).astype(jnp.int32)
    return jnp.where(rel > 0, half, 0) + jnp.where(n < max_exact, n, large)


def _bias_geometry(tb, tk, n_sub):
    assert tk % tb == 0
    ratio = tk // tb
    e_lo = (-(MAX_DISTANCE - 1) - tk) // tb
    e_hi = -((-(MAX_DISTANCE - 1) - tb) // tb)
    spans = [-(-(j + n_sub - 1 + e_hi) // ratio) - 1 - (j + e_lo) // ratio
             for j in range(0, ratio * n_sub, n_sub)]
    return e_lo, e_hi, max(spans) + max(spans) % 2


def _diff_kernel(rb_ref, lam_ref, gs_ref, q_ref, k_ref, v_ref, o_ref,
                 qz_ref, m_ref, acc_ref, s0_ref, s1_ref, mc0_ref, mc1_ref, bias_ref,
                 *, t, tk, lam_init):
    h = pl.program_id(0)
    bi = pl.program_id(1)
    i = pl.program_id(2)
    n = k_ref.shape[1]
    nc = n // tk
    tb = BLOCK_ROWS
    n_sub = t // tb
    ratio = tk // tb
    e_lo, e_hi, near_chunks = _bias_geometry(tb, tk, n_sub)
    shift_before = rb_ref[NUM_BUCKETS // 2 - 1, h] * LOG2E
    shift_after = rb_ref[NUM_BUCKETS - 1, h] * LOG2E

    @pl.when(jnp.logical_and(bi == 0, i == 0))
    def _():
        width = (e_hi - 1 - e_lo) * tb + tk
        bucket = _rel_bucket(e_lo * tb + lax.broadcasted_iota(jnp.int32, (8, width), 1))
        gen = jnp.zeros((8, width), F32)
        for bkt in range(NUM_BUCKETS):
            gen = jnp.where(bucket == bkt, rb_ref[bkt, h] * LOG2E, gen)
        sub = lax.broadcasted_iota(jnp.int32, (8, width), 0)
        rows8 = gen
        for r in range(1, 8):
            rows8 = jnp.where(sub == r, pltpu.roll(gen, r, 1), rows8)
        for r0 in range(0, tb, 8):
            blk = rows8 if r0 == 0 else pltpu.roll(rows8, r0, 1)
            for e in range(e_lo + 1, e_hi):
                off = (e - e_lo) * tb
                bias_ref[e - e_lo, r0:r0 + 8, :] = blk[:, off:off + tk]
        bias_ref[0] = jnp.full((tb, tk), shift_before, F32)
        bias_ref[e_hi - e_lo] = jnp.full((tb, tk), shift_after, F32)

    lane = lax.broadcasted_iota(jnp.int32, (t, LANES), 1)
    low = lane < HEAD_DIM
    q = q_ref[0]
    zero = jnp.zeros_like(q)
    qz_ref[0:t, :] = jnp.where(low, q, zero)
    qz_ref[t:2 * t, :] = jnp.where(low, zero, q)
    m_ref[...] = jnp.full(m_ref.shape, M_INIT, F32)
    acc_ref[...] = jnp.zeros(acc_ref.shape, F32)

    n_near = min(near_chunks, nc)
    near0 = jnp.clip(jnp.floor_divide(i * n_sub + e_lo, ratio) + 1, 0, nc - n_near)

    def tail_bias(c, rows):
        block = i * n_sub + (rows.start % t) // tb
        return bias_ref[jnp.clip(ratio * c - block, e_lo, e_hi) - e_lo]

    _flash_sweep(
        nc, tk, qz_ref, k_ref, v_ref,
        tuple(slice(r, r + BLOCK_ROWS) for r in range(0, 2 * t, BLOCK_ROWS)),
        ((s0_ref, mc0_ref), (s1_ref, mc1_ref)), m_ref, acc_ref,
        v_ones=jnp.ones((tk, LANES), BF16), n_tail=n_near,
        main_chunk=lambda pos: pos + jnp.where(pos >= near0, n_near, 0),
        main_shift=lambda c: jnp.where(c < near0, shift_before, shift_after),
        tail_chunk=lambda u: near0 + u,
        tail_bias=tail_bias)

    lv = lam_ref[...]
    lam = (jnp.exp(jnp.sum(lv[0:1, :] * lv[1:2, :], axis=-1, keepdims=True))
           - jnp.exp(jnp.sum(lv[2:3, :] * lv[3:4, :], axis=-1, keepdims=True)) + lam_init)
    o1 = acc_ref[0:t, 0:LANES] / acc_ref[0:t, LANES:2 * LANES]
    o2 = acc_ref[t:2 * t, 0:LANES] / acc_ref[t:2 * t, LANES:2 * LANES]
    o = o1 - lam * o2
    o_ref[0] = (_rms(o, gs_ref[...]) * (1.0 - lam_init)).astype(o_ref.dtype)


def _diff_attention(act, rel_bias, lam_rows, g_subln, lam_init, t=1024, tk=1024):
    b, n, _ = act.shape
    e_lo, e_hi, _ = _bias_geometry(BLOCK_ROWS, tk, t // BLOCK_ROWS)
    return pl.pallas_call(
        functools.partial(_diff_kernel, t=t, tk=tk, lam_init=lam_init),
        grid=(N_DIFF_HEADS, b, n // t),
        in_specs=[
            pl.BlockSpec(memory_space=pltpu.SMEM),
            pl.BlockSpec((MOD_ROWS, LANES), lambda h, bi, i: (0, 0)),
            pl.BlockSpec((1, LANES), lambda h, bi, i: (0, 0)),
            pl.BlockSpec((1, t, LANES), lambda h, bi, i: (bi, i, QD_BLK + h)),
            pl.BlockSpec((1, n, LANES), lambda h, bi, i: (bi, 0, KD_BLK + h),
                         pipeline_mode=pl.Buffered(1)),
            pl.BlockSpec((1, n, LANES), lambda h, bi, i: (bi, 0, VD_BLK + h),
                         pipeline_mode=pl.Buffered(1)),
        ],
        out_specs=pl.BlockSpec((1, t, LANES), lambda h, bi, i: (bi, i, h)),
        out_shape=jax.ShapeDtypeStruct((b, n, DIFF_COLS), BF16),
        scratch_shapes=_flash_scratch(2 * t, tk, 2 * LANES)
        + [pltpu.VMEM((e_hi - e_lo + 1, BLOCK_ROWS, tk), F32)],
        compiler_params=pltpu.CompilerParams(
            dimension_semantics=("arbitrary", "arbitrary", "arbitrary"),
            vmem_limit_bytes=VMEM_LIMIT),
        name="diff_attention",
    )(rel_bias, lam_rows, g_subln, act, act, act)


def _ffn_kernel(x_ref, oa_ref, od_ref, mod_ref, wout_ref, gpm_ref, gpf_ref, wgu_ref, wdn_ref,
                gpo_ref, y_ref):
    gt1 = mod_ref[0, 2:3, :]
    sh2 = mod_ref[0, 3:4, :]
    sc2 = mod_ref[0, 4:5, :]
    gt2 = mod_ref[0, 5:6, :]
    wa = oa_ref.shape[2]
    d_ff = wdn_ref.shape[0]
    tm = x_ref.shape[1]
    groups = [slice(r0, r0 + FFN_GROUP_ROWS) for r0 in range(0, tm, FFN_GROUP_ROWS)]
    mixes = [jnp.dot(oa_ref[0, rows, :], wout_ref[0:wa, :], preferred_element_type=F32)
             + jnp.dot(od_ref[0, rows, :], wout_ref[wa:, :], preferred_element_type=F32)
             for rows in groups]
    x1s, gus, fs = [], [], []
    for rows, mix in zip(groups, mixes):
        x1 = x_ref[0, rows, :] + gt1 * _rms(mix, gpm_ref[...])
        h = _rms(x1, gpf_ref[...]) * (1.0 + sc2) + sh2
        x1s.append(x1)
        gus.append(jnp.dot(h.astype(BF16), wgu_ref[...], preferred_element_type=F32))
    for gu in gus:
        gate = gu[:, :d_ff]
        up = gu[:, d_ff:]
        act = gate * (1.0 / (1.0 + jnp.exp(-gate))) * up
        fs.append(jnp.dot(act.astype(BF16), wdn_ref[...], preferred_element_type=F32))
    for rows, x1, f in zip(groups, x1s, fs):
        y_ref[0, rows, :] = x1 + gt2 * _rms(f, gpo_ref[...])


def _out_ffn(x, out_a, out_d, mod3, w_out_bf, g_post_mix, g_pre_ffn, w_gu_bf, w_down_bf,
             g_post_ffn, tm=512):
    b, n, d = x.shape
    const = lambda bi, i: (0, 0)
    once = pl.Buffered(1)
    return pl.pallas_call(
        _ffn_kernel,
        grid=(b, n // tm),
        in_specs=[
            pl.BlockSpec((1, tm, d), lambda bi, i: (bi, i, 0)),
            pl.BlockSpec((1, tm, out_a.shape[2]), lambda bi, i: (bi, i, 0)),
            pl.BlockSpec((1, tm, out_d.shape[2]), lambda bi, i: (bi, i, 0)),
            pl.BlockSpec((1, 6, d), lambda bi, i: (bi, 0, 0)),
            pl.BlockSpec(w_out_bf.shape, const, pipeline_mode=once),
            pl.BlockSpec((1, d), const),
            pl.BlockSpec((1, d), const),
            pl.BlockSpec(w_gu_bf.shape, const, pipeline_mode=once),
            pl.BlockSpec(w_down_bf.shape, const, pipeline_mode=once),
            pl.BlockSpec((1, d), const),
        ],
        out_specs=pl.BlockSpec((1, tm, d), lambda bi, i: (bi, i, 0)),
        out_shape=jax.ShapeDtypeStruct((b, n, d), F32),
        compiler_params=pltpu.CompilerParams(
            dimension_semantics=("arbitrary", "arbitrary"), vmem_limit_bytes=VMEM_LIMIT),
        name="out_ffn",
    )(x, out_a, out_d, mod3, w_out_bf, g_post_mix, g_pre_ffn, w_gu_bf, w_down_bf, g_post_ffn)


def _rope_tables(n):
    rows = n // GRID_W
    half = HEAD_DIM // 2
    quarter = half // 2
    lane = jnp.arange(LANES)
    inv_lane = (ROPE_THETA ** (-(2 * (lane % quarter)).astype(F32) / half))[None, :]
    ang_row = jnp.arange(rows, dtype=F32)[:, None] * inv_lane
    ang_col = jnp.arange(GRID_W, dtype=F32)[:, None] * inv_lane
    is_row = (lane % HEAD_DIM < half)[None, None, :]
    sign = jnp.where(lane % half < quarter, -1.0, 1.0).astype(F32)

    def table(fn, scale):
        by_row = (fn(ang_row) * scale)[:, None, :]
        by_col = (fn(ang_col) * scale)[None, :, :]
        return jnp.where(is_row, by_row, by_col).reshape(n, LANES)

    return table(jnp.cos, 1.0), table(jnp.sin, sign[None, :])


def kernel(x_prompt, x_sample, c_prompt, c_sample, rel_bias, w_ada, b_ada, g_pre_mix, w_in,
           g_q, g_k, lam_q1, lam_k1, lam_q2, lam_k2, g_subln, w_out, g_post_mix, g_pre_ffn,
           w_gu, w_down, g_post_ffn):
    depth = w_ada.shape[0]
    d = x_prompt.shape[-1]
    xs = [x_prompt, x_sample]
    cs = [c_prompt, c_sample]
    n_c = sum(c.shape[0] for c in cs)
    assert n_c <= MOD_ROWS
    c_all = jnp.concatenate(cs + [jnp.zeros((MOD_ROWS - n_c, d), F32)], axis=0)
    seg = (jnp.arange(LANES)[:, None] // HEAD_DIM
           == jnp.arange(LANES)[None, :] // HEAD_DIM).astype(BF16)
    tables = [_rope_tables(max(x.shape[1] for x in xs))] * len(xs)

    for l in range(depth):
        lam_init = 0.8 - 0.6 * math.exp(-0.3 * l)
        mod = _modulation(c_all, w_ada, b_ada[l], l).reshape(MOD_ROWS, 6, d)
        w_in_bf = w_in[l].astype(BF16)
        w_out_bf = w_out[l].astype(BF16)
        w_gu_bf = w_gu[l].astype(BF16)
        w_down_bf = w_down[l].astype(BF16)
        gq2 = jnp.tile(g_q[l], LANES // HEAD_DIM).reshape(1, LANES)
        gk2 = jnp.tile(g_k[l], LANES // HEAD_DIM).reshape(1, LANES)
        lam_rows = jnp.zeros((MOD_ROWS, LANES), F32).at[0:4, 0:HEAD_DIM].set(
            jnp.stack([lam_q1[l], lam_k1[l], lam_q2[l], lam_k2[l]]))
        new_xs = []
        row0 = 0
        for x, (cos2, sin2) in zip(xs, tables):
            mod3 = mod[row0:row0 + x.shape[0]]
            row0 += x.shape[0]
            act = _in_projection(x, mod3, g_pre_mix[l].reshape(1, d), w_in_bf, gq2, gk2,
                                 cos2, sin2, seg)
            out_a = _gqa_attention(act)
            out_d = _diff_attention(act, rel_bias, lam_rows, g_subln[l].reshape(1, LANES),
                                    lam_init)
            new_xs.append(_out_ffn(x, out_a, out_d, mod3, w_out_bf,
                                   g_post_mix[l].reshape(1, d), g_pre_ffn[l].reshape(1, d),
                                   w_gu_bf, w_down_bf, g_post_ffn[l].reshape(1, d)))
        xs = new_xs
    return tuple(xs)
```
